```python
import math
import jax, jax.numpy as jnp
from jax import lax
import numpy as np

D_MODEL = 1024
BATCH = 4
SEQ = 4096
DEPTH = 4

HEAD_DIM = 64
D_MIX = D_MODEL
W_FNET = D_MIX // 4
FNET_GROUPS = W_FNET // HEAD_DIM
N_HEADS_GQA = D_MIX // 4 // HEAD_DIM
N_KV_GQA = 2
W_CONV = D_MIX // 4
CONV_WIDTH = 31
N_HEADS_DIL = D_MIX // 4 // HEAD_DIM
DIL_CONFIGS = ((128, 1), (512, 4), (2048, 16))
D_FF = 4 * D_MODEL
GRID_W = 64
ROPE_THETA = 10000.0
Q_BLOCK = 128
REL_BUCKETS = 32
REL_MAX_DIST = 1024
ALPHA = (2 * DEPTH) ** 0.25
BETA = (8 * DEPTH) ** -0.25
LN_EPS = 1e-5
RMS_EPS = 1e-6
NEG = -1e30
IN_WIDTHS = (W_FNET, N_HEADS_GQA * HEAD_DIM, N_KV_GQA * HEAD_DIM, N_KV_GQA * HEAD_DIM,
             2 * W_CONV, N_HEADS_DIL * HEAD_DIM, N_HEADS_DIL * HEAD_DIM, N_HEADS_DIL * HEAD_DIM)
D_IN = sum(IN_WIDTHS)

kernel_name = "hybrid_parallel_fnet_gqa_conv_dilated_encoder"


def split_points():
    pts, acc = [], 0
    for w in IN_WIDTHS[:-1]:
        acc += w
        pts.append(acc)
    return pts


def layer_norm(x, g, b):
    xf = x.astype(jnp.float32)
    mu = xf.mean(-1, keepdims=True)
    var = jnp.square(xf - mu).mean(-1, keepdims=True)
    return ((xf - mu) * lax.rsqrt(var + LN_EPS) * g.astype(jnp.float32) + b.astype(jnp.float32)).astype(x.dtype)


def rms_norm(x, g):
    xf = x.astype(jnp.float32)
    return (xf * lax.rsqrt(jnp.mean(xf * xf, -1, keepdims=True) + RMS_EPS) * g.astype(jnp.float32)).astype(x.dtype)


def axial_rope_tables(S):
    rows = S // GRID_W
    row = jnp.repeat(jnp.arange(rows), GRID_W).astype(jnp.float32)
    col = jnp.tile(jnp.arange(GRID_W), rows).astype(jnp.float32)
    nf = HEAD_DIM // 4
    inv = ROPE_THETA ** (-jnp.arange(nf, dtype=jnp.float32) / nf)
    ang = jnp.concatenate([row[:, None] * inv, col[:, None] * inv], -1)
    return jnp.cos(ang), jnp.sin(ang)


def apply_axial_rope(x, cos, sin):
    B, S, H, hd = x.shape
    nf = hd // 4
    xs = x.astype(jnp.float32).reshape(B, S, H, 2, 2, nf)
    x1, x2 = xs[..., 0, :], xs[..., 1, :]
    c = cos.reshape(S, 1, 2, nf)
    s = sin.reshape(S, 1, 2, nf)
    out = jnp.stack([x1 * c - x2 * s, x1 * s + x2 * c], axis=-2).reshape(B, S, H, hd)
    return out.astype(x.dtype)


def t5_bucket(rel):
    nb = REL_BUCKETS // 2
    max_exact = nb // 2
    ret = jnp.where(rel > 0, nb, 0)
    n = jnp.abs(rel)
    nf = jnp.maximum(n, 1).astype(jnp.float32)
    large = max_exact + (jnp.log(nf / max_exact) / math.log(REL_MAX_DIST / max_exact)
                         * (nb - max_exact)).astype(jnp.int32)
    large = jnp.minimum(large, nb - 1)
    return ret + jnp.where(n < max_exact, n, large)


def fourier_mix(u, w):
    B, S, _ = u.shape
    a = u.astype(jnp.float32).reshape(B, S, FNET_GROUPS, HEAD_DIM)
    f = jnp.fft.fft2(a, axes=(1, 3), norm='ortho').real
    return f.reshape(B, S, W_FNET).astype(u.dtype) @ w


def gqa_attention(q, k, v):
    B, S, H, hd = q.shape
    G = k.shape[2]
    R = H // G
    nqb = S // Q_BLOCK
    qb = q.reshape(B, nqb, Q_BLOCK, G, R, hd).transpose(1, 0, 2, 3, 4, 5)
    scale = hd ** -0.5

    def block(qblk):
        s = jnp.einsum('bqgrd,bkgd->bgrqk', qblk, k).astype(jnp.float32) * scale
        p = jax.nn.softmax(s, axis=-1).astype(v.dtype)
        return jnp.einsum('bgrqk,bkgd->bqgrd', p, v)

    o = lax.map(block, qb)
    return o.transpose(1, 0, 2, 3, 4, 5).reshape(B, S, H * hd)


def conformer_conv(u, dw, b, g, beta, w_pw):
    a, gate = jnp.split(u, 2, axis=-1)
    h = a * jax.nn.sigmoid(gate)
    pad = CONV_WIDTH // 2
    h = lax.conv_general_dilated(h, dw[:, None, :], window_strides=(1,), padding=[(pad, pad)],
                                 dimension_numbers=('NWC', 'WIO', 'NWC'),
                                 feature_group_count=W_CONV) + b
    h = jax.nn.silu(layer_norm(h, g, beta))
    return h @ w_pw


def dilated_branch(q, k, v, dil, n, rel_bias):
    B, S, H, hd = q.shape
    L = S // dil

    def to_res(t):
        return t.reshape(B, L, dil, H, hd).transpose(0, 2, 1, 3, 4).reshape(B * dil, L, H, hd)

    qr, kr, vr = to_res(q), to_res(k), to_res(v)
    nb = -(-L // n)
    Lp = nb * n
    qr = jnp.pad(qr, ((0, 0), (0, Lp - L), (0, 0), (0, 0)))
    kp = jnp.pad(kr, ((0, 0), (n, Lp - L + n), (0, 0), (0, 0)))
    vp = jnp.pad(vr, ((0, 0), (n, Lp - L + n), (0, 0), (0, 0)))

    def band(t):
        tb = t.reshape(B * dil, nb + 2, n, H, hd)
        return jnp.concatenate([tb[:, :-2], tb[:, 1:-1], tb[:, 2:]], axis=2)

    kw, vw = band(kp), band(vp)
    qb = qr.reshape(B * dil, nb, n, H, hd)
    s = jnp.einsum('bnqhd,bnkhd->bnhqk', qb, kw).astype(jnp.float32) * (hd ** -0.5)
    qi = jnp.arange(n)
    ki = jnp.arange(3 * n)
    rel = ki[None, :] - n - qi[:, None]
    kpos = jnp.arange(nb)[:, None] * n - n + ki[None, :]
    valid = (jnp.abs(rel) <= n)[None] & ((kpos >= 0) & (kpos < L))[:, None, :]
    bias = rel_bias[t5_bucket(rel * dil)].astype(jnp.float32).transpose(2, 0, 1)
    logits = jnp.where(valid[None, :, None], s + bias[None, None], NEG)
    m = logits.max(-1, keepdims=True)
    p = jnp.exp(logits - m)
    den = p.sum(-1, keepdims=True)
    o = jnp.einsum('bnhqk,bnkhd->bnqhd', (p / den).astype(v.dtype), vw)
    lse = (m + jnp.log(den))[..., 0].transpose(0, 1, 3, 2)
    o = o.reshape(B * dil, Lp, H, hd)[:, :L]
    lse = lse.reshape(B * dil, Lp, H)[:, :L]

    def from_res(t):
        return t.reshape(B, dil, L, *t.shape[2:]).swapaxes(1, 2).reshape(B, S, *t.shape[2:])

    return from_res(o), from_res(lse)


def dilated_mixture(q, k, v, rel_bias):
    outs, lses = [], []
    for window, dil in DIL_CONFIGS:
        o, lse = dilated_branch(q, k, v, dil, window // (2 * dil), rel_bias)
        outs.append(o)
        lses.append(lse)
    w = jax.nn.softmax(jnp.stack(lses, 0), axis=0)
    out = jnp.sum(w[..., None].astype(q.dtype) * jnp.stack(outs, 0), axis=0)
    B, S, H, hd = q.shape
    return out.reshape(B, S, H * hd)


def setup_inputs(seed: int = 0) -> dict:
    key = jax.random.key(seed)
    ks = jax.random.split(key, 24)
    f32 = jnp.float32
    nrm = lambda k, shape: jax.random.normal(k, shape, f32)
    return {
        'x': nrm(ks[0], (BATCH, SEQ, D_MODEL)),
        'emb_ln_g': 1.0 + 0.02 * nrm(ks[1], (D_MODEL,)),
        'emb_ln_b': 0.02 * nrm(ks[2], (D_MODEL,)),
        'w_in': nrm(ks[3], (DEPTH, D_MODEL, D_IN)) * D_MODEL ** -0.5,
        'w_fnet': nrm(ks[4], (DEPTH, W_FNET, W_FNET)) * W_FNET ** -0.5,
        'q_norm_g': 1.0 + 0.02 * nrm(ks[5], (DEPTH, HEAD_DIM)),
        'k_norm_g': 1.0 + 0.02 * nrm(ks[6], (DEPTH, HEAD_DIM)),
        'conv_dw': nrm(ks[7], (DEPTH, CONV_WIDTH, W_CONV)) * CONV_WIDTH ** -0.5,
        'conv_b': 0.02 * nrm(ks[8], (DEPTH, W_CONV)),
        'conv_ln_g': 1.0 + 0.02 * nrm(ks[9], (DEPTH, W_CONV)),
        'conv_ln_b': 0.02 * nrm(ks[10], (DEPTH, W_CONV)),
        'w_conv_out': nrm(ks[11], (DEPTH, W_CONV, W_CONV)) * W_CONV ** -0.5,
        'w_out': nrm(ks[12], (DEPTH, D_MIX, D_MODEL)) * (D_MIX ** -0.5 * BETA),
        'ln1_g': 1.0 + 0.02 * nrm(ks[13], (DEPTH, D_MODEL)),
        'ln1_b': 0.02 * nrm(ks[14], (DEPTH, D_MODEL)),
        'w_ff1': nrm(ks[15], (DEPTH, D_MODEL, D_FF)) * D_MODEL ** -0.5,
        'w_ff2': nrm(ks[16], (DEPTH, D_FF, D_MODEL)) * (D_FF ** -0.5 * BETA),
        'ln2_g': 1.0 + 0.02 * nrm(ks[17], (DEPTH, D_MODEL)),
        'ln2_b': 0.02 * nrm(ks[18], (DEPTH, D_MODEL)),
        'rel_bias': 0.2 * nrm(ks[19], (REL_BUCKETS, N_HEADS_DIL)),
    }


def reference(x, emb_ln_g, emb_ln_b, w_in, w_fnet, q_norm_g, k_norm_g, conv_dw, conv_b,
              conv_ln_g, conv_ln_b, w_conv_out, w_out, ln1_g, ln1_b, w_ff1, w_ff2,
              ln2_g, ln2_b, rel_bias):
    B, S, _ = x.shape
    cos, sin = axial_rope_tables(S)
    pts = split_points()
    x = layer_norm(x, emb_ln_g, emb_ln_b)
    for l in range(DEPTH):
        u = x @ w_in[l]
        u_f, q_b, k_b, v_b, u_c, q_d, k_d, v_d = jnp.split(u, pts, axis=-1)
        y_a = fourier_mix(u_f, w_fnet[l])
        qh = apply_axial_rope(rms_norm(q_b.reshape(B, S, N_HEADS_GQA, HEAD_DIM), q_norm_g[l]), cos, sin)
        kh = apply_axial_rope(rms_norm(k_b.reshape(B, S, N_KV_GQA, HEAD_DIM), k_norm_g[l]), cos, sin)
        vh = v_b.reshape(B, S, N_KV_GQA, HEAD_DIM)
        y_b = gqa_attention(qh, kh, vh)
        y_c = conformer_conv(u_c, conv_dw[l], conv_b[l], conv_ln_g[l], conv_ln_b[l], w_conv_out[l])
        y_d = dilated_mixture(q_d.reshape(B, S, N_HEADS_DIL, HEAD_DIM),
                              k_d.reshape(B, S, N_HEADS_DIL, HEAD_DIM),
                              v_d.reshape(B, S, N_HEADS_DIL, HEAD_DIM), rel_bias)
        y = jnp.concatenate([y_a, y_b, y_c, y_d], axis=-1) @ w_out[l]
        x = layer_norm(ALPHA * x + y, ln1_g[l], ln1_b[l])
        h = jnp.square(jax.nn.relu(x @ w_ff1[l]))
        x = layer_norm(ALPHA * x + h @ w_ff2[l], ln2_g[l], ln2_b[l])
    return x
```

```python
import functools
import math

import numpy as np
import jax
import jax.numpy as jnp
from jax import lax
from jax.experimental import pallas as pl
from jax.experimental.pallas import tpu as pltpu

F32 = jnp.float32
BF16 = jnp.bfloat16

D_MODEL = 1024
DEPTH = 4
HEAD_DIM = 64
W_MIX = 256
N_KV_GQA = 2
CONV_WIDTH = 31
CONV_PAD = CONV_WIDTH // 2
DIL_CONFIGS = ((128, 1), (512, 4), (2048, 16))
DIL_N = 64
D_FF = 4 * D_MODEL
GRID_W = 64
ROPE_THETA = 10000.0
REL_BUCKETS = 32
REL_MAX_DIST = 1024
ALPHA = (2 * DEPTH) ** 0.25
LN_EPS = 1e-5
RMS_EPS = 1e-6
NEG = -1e30
ATTN_SCALE = HEAD_DIM ** -0.5

COL_QB, COL_KVB, COL_GLU, COL_QD, COL_KD, COL_VD, D_IN = 256, 512, 768, 1280, 1536, 1792, 2048

FFT_R = 64
HALO = 16
VMEM_LIMIT = 56 * 1024 * 1024


def _cparams(sem):
    return pltpu.CompilerParams(dimension_semantics=sem, vmem_limit_bytes=VMEM_LIMIT)


def _dft_cos_sin(n):
    j = np.arange(n)
    ang = 2.0 * np.pi * ((j[:, None] * j[None, :]) % n) / n
    return np.cos(ang), np.sin(ang)


def _channel_dft_blockdiag(seq):
    c, s = _dft_cos_sin(HEAD_DIM)
    scale = 1.0 / math.sqrt(seq * HEAD_DIM)
    eye = np.eye(W_MIX // HEAD_DIM)
    return (np.kron(eye, c) * scale).astype(np.float32), (np.kron(eye, -s) * scale).astype(np.float32)


def _fft_stage_mats(a):
    ca, sa = _dft_cos_sin(a)
    g1 = np.block([[ca, sa], [-sa, ca]])
    c2, s2 = _dft_cos_sin(FFT_R)
    g2 = np.concatenate([c2, s2], axis=1)
    return g1.astype(np.float32), g2.astype(np.float32)


def _twiddle(a, seq):
    c = np.arange(a)[:, None]
    b = np.arange(FFT_R)[None, :]
    ang = 2.0 * np.pi * ((b * c) % seq) / seq
    return np.cos(ang).astype(np.float32), (-np.sin(ang)).astype(np.float32)


def _rope_tables(seq):
    rows = seq // GRID_W
    row = np.repeat(np.arange(rows), GRID_W).astype(np.float32)
    col = np.tile(np.arange(GRID_W), rows).astype(np.float32)
    nf = HEAD_DIM // 4
    inv = (ROPE_THETA ** (-np.arange(nf, dtype=np.float32) / nf)).astype(np.float32)
    ar = row[:, None] * inv
    ac = col[:, None] * inv
    cos = np.concatenate([np.cos(ar), np.cos(ar), np.cos(ac), np.cos(ac)], -1)
    sin = np.concatenate([-np.sin(ar), np.sin(ar), -np.sin(ac), np.sin(ac)], -1)
    reps = W_MIX // HEAD_DIM
    return np.tile(cos, (1, reps)).astype(np.float32), np.tile(sin, (1, reps)).astype(np.float32)


def _t5_bucket_np(rel):
    nb = REL_BUCKETS // 2
    max_exact = nb // 2
    ret = np.where(rel > 0, nb, 0)
    n = np.abs(rel)
    nf = np.maximum(n, 1).astype(np.float32)
    large = max_exact + (np.log(nf / np.float32(max_exact)) / np.float32(math.log(REL_MAX_DIST / max_exact))
                         * np.float32(nb - max_exact)).astype(np.int32)
    large = np.minimum(large, nb - 1)
    return ret + np.where(n < max_exact, n, large)


def _bucket_tiles(tq):
    qi = np.arange(tq)[:, None]
    ki = np.arange(tq + 2 * DIL_N)[None, :]
    rel = ki - DIL_N - qi
    tiles = []
    for _, dil in DIL_CONFIGS:
        tiles.append(np.where(np.abs(rel) <= DIL_N, _t5_bucket_np(rel * dil), -1))
    return np.stack(tiles).astype(np.int32)


def _layer_norm(r, g, b):
    mu = jnp.mean(r, axis=-1, keepdims=True)
    d = r - mu
    var = jnp.mean(d * d, axis=-1, keepdims=True)
    return d * lax.rsqrt(var + LN_EPS) * g + b


def _dot(a, b):
    return jnp.dot(a, b, preferred_element_type=F32)


def _dot_nt(a, b):
    return lax.dot_general(a, b, (((1,), (1,)), ((), ())), preferred_element_type=F32)


def _norm_rope(x, g, cos, sin, ones):
    x2 = x * x
    hi = x2.astype(BF16)
    lo = (x2 - hi.astype(F32)).astype(BF16)
    ss = _dot(hi, ones) + _dot(lo, ones)
    xn = x * lax.rsqrt(ss * (1.0 / HEAD_DIM) + RMS_EPS) * g
    w = x.shape[1]
    lane = lax.broadcasted_iota(jnp.int32, x.shape, 1)
    quarter = HEAD_DIM // 4
    partner = jnp.where((lane % (2 * quarter)) < quarter,
                        pltpu.roll(xn, w - quarter, 1), pltpu.roll(xn, quarter, 1))
    return xn * cos + partner * sin


def _embed_ln_kernel(x_ref, g_ref, b_ref, o_ref):
    o_ref[...] = _layer_norm(x_ref[...], g_ref[...], b_ref[...])


def _fold_kernel(win_ref, wf_ref, c_ref, s_ref, o_ref):
    hp = lax.Precision.HIGHEST
    wf = wf_ref[...]
    mc = jnp.dot(c_ref[...], wf, precision=hp, preferred_element_type=F32)
    ms = jnp.dot(s_ref[...], wf, precision=hp, preferred_element_type=F32)
    win = win_ref[...]
    o_ref[:, :W_MIX] = jnp.dot(win, mc, precision=hp, preferred_element_type=F32).astype(BF16)
    o_ref[:, W_MIX:] = jnp.dot(win, ms, precision=hp, preferred_element_type=F32).astype(BF16)


def _inproj_kernel(x_ref, wfold_ref, w_ref, cos_ref, sin_ref, gq_ref, gk_ref, ones_ref,
                   zr_ref, zi_ref, qn_ref, kn_ref, v_ref, uc_ref, qd_ref, kd_ref, vd_ref):
    xb = x_ref[...].astype(BF16)
    z = _dot(xb, wfold_ref[...])
    zr_ref[...] = z[:, :W_MIX].astype(BF16)
    zi_ref[...] = z[:, W_MIX:].astype(BF16)
    cos = cos_ref[...]
    sin = sin_ref[...]
    ones = ones_ref[...]
    q = _dot(xb, w_ref[:, COL_QB:COL_KVB])
    qn_ref[...] = (_norm_rope(q, gq_ref[...], cos, sin, ones) * ATTN_SCALE).astype(BF16)
    kvw = N_KV_GQA * HEAD_DIM
    kv = _dot(xb, w_ref[:, COL_KVB:COL_GLU])
    kn_ref[...] = _norm_rope(kv[:, :kvw], gk_ref[...], cos[:, :kvw], sin[:, :kvw], ones[:kvw, :kvw]).astype(BF16)
    v_ref[...] = kv[:, kvw:].astype(BF16)
    uc_ref[...] = _dot(xb, w_ref[:, COL_GLU:COL_QD])
    qd_ref[...] = _dot(xb, w_ref[:, COL_QD:COL_KD]).astype(BF16)
    kd_ref[...] = _dot(xb, w_ref[:, COL_KD:COL_VD]).astype(BF16)
    vd_ref[...] = _dot(xb, w_ref[:, COL_VD:D_IN]).astype(BF16)


def _fft1_kernel(zr_ref, zi_ref, g1_ref, twr_ref, twi_ref, yr_ref, yi_ref, *, a, nb):
    z = jnp.concatenate([zr_ref[...], zi_ref[...]], axis=0)
    y = _dot(g1_ref[...], z)
    yr, yi = y[:a], y[a:]
    twr, twi = twr_ref[...], twi_ref[...]
    pr = yr * twr - yi * twi
    pi = yr * twi + yi * twr
    for j in range(nb):
        yr_ref[j] = pr[:, j * W_MIX:(j + 1) * W_MIX].astype(BF16)
        yi_ref[j] = pi[:, j * W_MIX:(j + 1) * W_MIX].astype(BF16)


def _fft2_kernel(yr_ref, yi_ref, g2_ref, o_ref):
    y = jnp.concatenate([yr_ref[...], yi_ref[...]], axis=0)
    o_ref[...] = _dot(g2_ref[...], y).astype(BF16)


def _gqa_kernel(q_ref, k_ref, v_ref, o_ref, *, tq, tk, seq):
    nk = seq // tk
    rep = (W_MIX // HEAD_DIM) // N_KV_GQA
    for g in range(N_KV_GQA):
        heads = [g * rep + r for r in range(rep)]
        q2 = jnp.concatenate([q_ref[:, h * HEAD_DIM:(h + 1) * HEAD_DIM] for h in heads], axis=0)
        lo, hi = g * HEAD_DIM, (g + 1) * HEAD_DIM

        def body(c, carry):
            m, l, acc = carry
            off = pl.multiple_of(c * tk, tk)
            kc = k_ref[pl.ds(off, tk), lo:hi]
            vc = v_ref[pl.ds(off, tk), lo:hi]
            s = _dot_nt(q2, kc)
            m_new = jnp.maximum(m, jnp.max(s, axis=-1, keepdims=True))
            alpha = jnp.exp(m - m_new)
            p = jnp.exp(s - m_new)
            l = alpha * l + jnp.sum(p, axis=-1, keepdims=True)
            acc = alpha * acc + _dot(p.astype(BF16), vc)
            return m_new, l, acc

        rows = rep * tq
        init = (jnp.full((rows, 1), -jnp.inf, F32), jnp.zeros((rows, 1), F32), jnp.zeros((rows, HEAD_DIM), F32))
        _, l, acc = lax.fori_loop(0, nk, body, init)
        o = acc / l
        for r, h in enumerate(heads):
            o_ref[:, h * HEAD_DIM:(h + 1) * HEAD_DIM] = o[r * tq:(r + 1) * tq].astype(BF16)


def _conv_kernel(cur_ref, prev_ref, next_ref, dw_ref, cb_ref, g_ref, b_ref, wpw_ref, o_ref, hpad, *, ts, nblk):
    c = pl.program_id(0) % nblk

    def glu(u):
        return u[:, :W_MIX] * jax.nn.sigmoid(u[:, W_MIX:])

    hpad[0:HALO, :] = jnp.where(c > 0, glu(prev_ref[...]), 0.0)
    hpad[HALO:HALO + ts, :] = glu(cur_ref[...])
    hpad[HALO + ts:HALO + ts + HALO, :] = jnp.where(c < nblk - 1, glu(next_ref[...]), 0.0)
    acc = jnp.zeros((ts, W_MIX), F32)
    for j in range(CONV_WIDTH):
        start = HALO - CONV_PAD + j
        acc = acc + dw_ref[j:j + 1, :] * hpad[start:start + ts, :]
    h = _layer_norm(acc + cb_ref[...], g_ref[...], b_ref[...])
    h = h * jax.nn.sigmoid(h)
    o_ref[...] = _dot(h.astype(BF16), wpw_ref[...]).astype(BF16)


def _bias_kernel(idx_ref, rb_ref, o_ref):
    idx = idx_ref[...]
    for h in range(W_MIX // HEAD_DIM):
        tile = jnp.full(idx.shape, NEG, F32)
        for b in range(REL_BUCKETS):
            tile = jnp.where(idx == b, rb_ref[b, h], tile)
        o_ref[h] = tile


def _dilated_kernel(q_ref, k_ref, v_ref, bias_ref, o_ref, lse_ref, kpad, vpad, *, length, tq):
    n = DIL_N
    win = tq + 2 * n
    zeros = jnp.zeros((n, W_MIX), BF16)
    kpad[0:n, :] = zeros
    vpad[0:n, :] = zeros
    kpad[n + length:n + length + n, :] = zeros
    vpad[n + length:n + length + n, :] = zeros
    kpad[n:n + length, :] = k_ref[...]
    vpad[n:n + length, :] = v_ref[...]

    def body(i, carry):
        m0 = pl.multiple_of(i * tq, tq)
        qb = q_ref[pl.ds(m0, tq), :]
        kw = kpad[pl.ds(m0, win), :]
        vw = vpad[pl.ds(m0, win), :]
        kpos = m0 - n + lax.broadcasted_iota(jnp.int32, (tq, win), 1)
        valid = (kpos >= 0) & (kpos < length)
        for h in range(W_MIX // HEAD_DIM):
            sl = slice(h * HEAD_DIM, (h + 1) * HEAD_DIM)
            s = _dot_nt(qb[:, sl], kw[:, sl]) * ATTN_SCALE + bias_ref[h]
            s = jnp.where(valid, s, NEG)
            m = jnp.max(s, axis=-1, keepdims=True)
            p = jnp.exp(s - m)
            den = jnp.sum(p, axis=-1, keepdims=True)
            o = _dot(p.astype(BF16), vw[:, sl]) / den
            lse = m + jnp.log(den)
            o_ref[pl.ds(m0, tq), sl] = o
            lse_ref[pl.ds(m0, tq), sl] = jnp.broadcast_to(lse, (tq, HEAD_DIM))
        return carry

    lax.fori_loop(0, length // tq, body, 0)


def _outproj_kernel(x_ref, ya_ref, yb_ref, yc_ref, o1_ref, o2_ref, o3_ref, l1_ref, l2_ref, l3_ref,
                    w_ref, g_ref, b_ref, out_ref):
    l1, l2, l3 = l1_ref[...], l2_ref[...], l3_ref[...]
    m = jnp.maximum(jnp.maximum(l1, l2), l3)
    e1, e2, e3 = jnp.exp(l1 - m), jnp.exp(l2 - m), jnp.exp(l3 - m)
    yd = (e1 * o1_ref[...] + e2 * o2_ref[...] + e3 * o3_ref[...]) / (e1 + e2 + e3)
    y = _dot(ya_ref[...], w_ref[0:W_MIX, :])
    y = y + _dot(yb_ref[...], w_ref[W_MIX:2 * W_MIX, :])
    y = y + _dot(yc_ref[...], w_ref[2 * W_MIX:3 * W_MIX, :])
    y = y + _dot(yd.astype(BF16), w_ref[3 * W_MIX:4 * W_MIX, :])
    out_ref[...] = _layer_norm(ALPHA * x_ref[...] + y, g_ref[...], b_ref[...])


def _ffn_kernel(x_ref, w1_ref, w2_ref, g_ref, b_ref, o_ref, acc_ref, xb_ref):
    j = pl.program_id(1)

    @pl.when(j == 0)
    def _():
        acc_ref[...] = jnp.zeros_like(acc_ref)
        xb_ref[...] = x_ref[...].astype(BF16)

    h = _dot(xb_ref[...], w1_ref[...])
    h = jnp.square(jnp.maximum(h, 0.0)).astype(BF16)
    acc_ref[...] += _dot(h, w2_ref[...])

    @pl.when(j == pl.num_programs(1) - 1)
    def _():
        o_ref[...] = _layer_norm(ALPHA * x_ref[...] + acc_ref[...], g_ref[...], b_ref[...])


def _tile_rows(n, want):
    t = min(n, want)
    assert n % t == 0, (n, t)
    return t


def _embed_ln(x, g, b):
    n, d = x.shape
    tm = _tile_rows(n, 512)
    return pl.pallas_call(
        _embed_ln_kernel, grid=(n // tm,), name="embed_ln",
        in_specs=[pl.BlockSpec((tm, d), lambda i: (i, 0)), pl.BlockSpec((1, d), lambda i: (0, 0)),
                  pl.BlockSpec((1, d), lambda i: (0, 0))],
        out_specs=pl.BlockSpec((tm, d), lambda i: (i, 0)),
        out_shape=jax.ShapeDtypeStruct((n, d), F32), compiler_params=_cparams(("parallel",)),
    )(x, g.reshape(1, d), b.reshape(1, d))


def _fold_fourier_weights(w_in, w_fnet, seq):
    c_bd, s_bd = _channel_dft_blockdiag(seq)
    depth = w_in.shape[0]
    sq = pl.BlockSpec((W_MIX, W_MIX), lambda l: (0, 0))
    return pl.pallas_call(
        _fold_kernel, grid=(depth,), name="fold_fourier_weights",
        in_specs=[pl.BlockSpec((None, D_MODEL, W_MIX), lambda l: (l, 0, 0)),
                  pl.BlockSpec((None, W_MIX, W_MIX), lambda l: (l, 0, 0)), sq, sq],
        out_specs=pl.BlockSpec((None, D_MODEL, 2 * W_MIX), lambda l: (l, 0, 0)),
        out_shape=jax.ShapeDtypeStruct((depth, D_MODEL, 2 * W_MIX), BF16),
        compiler_params=_cparams(("parallel",)),
    )(w_in, w_fnet, jnp.asarray(c_bd), jnp.asarray(s_bd))


def _inproj(x, wfold, w_in_bf, cos, sin, gq, gk, ones, layer, seq):
    n = x.shape[0]
    tm = _tile_rows(seq, 512)
    kvw = N_KV_GQA * HEAD_DIM
    pos_blocks = seq // tm

    def rows(c):
        return pl.BlockSpec((tm, c), lambda i: (i, 0))

    def whole(r, c):
        return pl.BlockSpec((r, c), lambda i: (0, 0))

    out_cols = (W_MIX, W_MIX, W_MIX, kvw, kvw, 2 * W_MIX, W_MIX, W_MIX, W_MIX)
    out_dtypes = (BF16, BF16, BF16, BF16, BF16, F32, BF16, BF16, BF16)
    return pl.pallas_call(
        _inproj_kernel, grid=(n // tm,), name="inproj",
        in_specs=[rows(D_MODEL),
                  pl.BlockSpec((None, D_MODEL, 2 * W_MIX), lambda i: (layer, 0, 0)),
                  pl.BlockSpec((None, D_MODEL, D_IN), lambda i: (layer, 0, 0)),
                  pl.BlockSpec((tm, W_MIX), lambda i: (i % pos_blocks, 0)),
                  pl.BlockSpec((tm, W_MIX), lambda i: (i % pos_blocks, 0)),
                  pl.BlockSpec((None, 1, W_MIX), lambda i: (layer, 0, 0)),
                  pl.BlockSpec((None, 1, kvw), lambda i: (layer, 0, 0)),
                  whole(W_MIX, W_MIX)],
        out_specs=[rows(c) for c in out_cols],
        out_shape=[jax.ShapeDtypeStruct((n, c), dt) for c, dt in zip(out_cols, out_dtypes)],
        compiler_params=_cparams(("parallel",)),
    )(x, wfold, w_in_bf, cos, sin, gq, gk, ones)


def _fourier(zr, zi, g1, g2, twr, twi, batch, seq):
    a = seq // FFT_R
    lanes = FFT_R * W_MIX
    nb = 16
    zr3 = zr.reshape(batch, a, lanes)
    zi3 = zi.reshape(batch, a, lanes)
    blk = pl.BlockSpec((None, a, nb * W_MIX), lambda b, j: (b, 0, j))
    tw = pl.BlockSpec((a, nb * W_MIX), lambda b, j: (0, j))
    y_spec = pl.BlockSpec((None, nb, a, W_MIX), lambda b, j: (b, j, 0, 0))
    y_shape = jax.ShapeDtypeStruct((batch, FFT_R, a, W_MIX), BF16)
    yr, yi = pl.pallas_call(
        functools.partial(_fft1_kernel, a=a, nb=nb), grid=(batch, FFT_R // nb), name="fourier_stage1",
        in_specs=[blk, blk, pl.BlockSpec((2 * a, 2 * a), lambda b, j: (0, 0)), tw, tw],
        out_specs=[y_spec, y_spec], out_shape=[y_shape, y_shape],
        compiler_params=_cparams(("parallel", "parallel")),
    )(zr3, zi3, g1, twr, twi)
    nc = min(a, 16)
    yr2 = yr.reshape(batch, FFT_R, a * W_MIX)
    yi2 = yi.reshape(batch, FFT_R, a * W_MIX)
    blk2 = pl.BlockSpec((None, FFT_R, nc * W_MIX), lambda b, j: (b, 0, j))
    out = pl.pallas_call(
        _fft2_kernel, grid=(batch, a // nc), name="fourier_stage2",
        in_specs=[blk2, blk2, pl.BlockSpec((FFT_R, 2 * FFT_R), lambda b, j: (0, 0))],
        out_specs=blk2, out_shape=jax.ShapeDtypeStruct((batch, FFT_R, a * W_MIX), BF16),
        compiler_params=_cparams(("parallel", "parallel")),
    )(yr2, yi2, g2)
    return out.reshape(batch * seq, W_MIX)


def _gqa(qn, kn, v, batch, seq):
    tq = _tile_rows(seq, 256)
    tk = _tile_rows(seq, 512)
    nq = seq // tq
    kvw = N_KV_GQA * HEAD_DIM
    return pl.pallas_call(
        functools.partial(_gqa_kernel, tq=tq, tk=tk, seq=seq), grid=(batch, nq), name="gqa_attention",
        in_specs=[pl.BlockSpec((tq, W_MIX), lambda b, i: (b * nq + i, 0)),
                  pl.BlockSpec((seq, kvw), lambda b, i: (b, 0)),
                  pl.BlockSpec((seq, kvw), lambda b, i: (b, 0))],
        out_specs=pl.BlockSpec((tq, W_MIX), lambda b, i: (b * nq + i, 0)),
        out_shape=jax.ShapeDtypeStruct((batch * seq, W_MIX), BF16),
        compiler_params=_cparams(("parallel", "parallel")),
    )(qn, kn, v)


def _conv(uc, dw, cb, g, b, wpw_bf, layer, seq):
    n = uc.shape[0]
    ts = _tile_rows(seq, 512)
    nblk = seq // ts
    hb = ts // HALO
    last = n // HALO - 1

    def vec():
        return pl.BlockSpec((None, 1, W_MIX), lambda i: (layer, 0, 0))

    return pl.pallas_call(
        functools.partial(_conv_kernel, ts=ts, nblk=nblk), grid=(n // ts,), name="conformer_conv",
        in_specs=[pl.BlockSpec((ts, 2 * W_MIX), lambda i: (i, 0)),
                  pl.BlockSpec((HALO, 2 * W_MIX), lambda i: (jnp.maximum(i * hb - 1, 0), 0)),
                  pl.BlockSpec((HALO, 2 * W_MIX), lambda i: (jnp.minimum((i + 1) * hb, last), 0)),
                  pl.BlockSpec((None, CONV_WIDTH, W_MIX), lambda i: (layer, 0, 0)),
                  vec(), vec(), vec(),
                  pl.BlockSpec((None, W_MIX, W_MIX), lambda i: (layer, 0, 0))],
        out_specs=pl.BlockSpec((ts, W_MIX), lambda i: (i, 0)),
        out_shape=jax.ShapeDtypeStruct((n, W_MIX), BF16),
        scratch_shapes=[pltpu.VMEM((ts + 2 * HALO, W_MIX), F32)],
        compiler_params=_cparams(("parallel",)),
    )(uc, uc, uc, dw, cb, g, b, wpw_bf)


def _bias_tiles(rel_bias, tq):
    idx = jnp.asarray(_bucket_tiles(tq))
    nbr, _, win = idx.shape
    heads = W_MIX // HEAD_DIM
    return pl.pallas_call(
        _bias_kernel, grid=(nbr,), name="dilated_bias_tiles",
        in_specs=[pl.BlockSpec((None, tq, win), lambda r: (r, 0, 0)),
                  pl.BlockSpec(memory_space=pltpu.SMEM)],
        out_specs=pl.BlockSpec((None, heads, tq, win), lambda r: (r, 0, 0, 0)),
        out_shape=jax.ShapeDtypeStruct((nbr, heads, tq, win), F32),
        compiler_params=_cparams(("parallel",)),
    )(idx, rel_bias)


def _dilated_branch(qd, kd, vd, bias, branch, dil, tq, batch, seq):
    length = seq // dil
    heads = W_MIX // HEAD_DIM
    win = tq + 2 * DIL_N

    def view(t):
        return t.reshape(batch, length, dil * W_MIX)

    blk = pl.BlockSpec((None, length, W_MIX), lambda b, r: (b, 0, r))
    o_shape = jax.ShapeDtypeStruct((batch, length, dil * W_MIX), F32)
    o, lse = pl.pallas_call(
        functools.partial(_dilated_kernel, length=length, tq=tq), grid=(batch, dil), name=f"dilated_d{dil}",
        in_specs=[blk, blk, blk, pl.BlockSpec((None, heads, tq, win), lambda b, r: (branch, 0, 0, 0))],
        out_specs=[blk, blk], out_shape=[o_shape, o_shape],
        scratch_shapes=[pltpu.VMEM((length + 2 * DIL_N, W_MIX), BF16), pltpu.VMEM((length + 2 * DIL_N, W_MIX), BF16)],
        compiler_params=_cparams(("parallel", "parallel")),
    )(view(qd), view(kd), view(vd), bias)
    return o.reshape(batch * seq, W_MIX), lse.reshape(batch * seq, W_MIX)


def _outproj(x, ya, yb, yc, outs, lses, w_out_bf, g, b, layer):
    n = x.shape[0]
    tm = _tile_rows(n, 512)

    def rows(c):
        return pl.BlockSpec((tm, c), lambda i: (i, 0))

    def vec():
        return pl.BlockSpec((None, 1, D_MODEL), lambda i: (layer, 0, 0))

    return pl.pallas_call(
        _outproj_kernel, grid=(n // tm,), name="outproj_ln",
        in_specs=[rows(D_MODEL)] + [rows(W_MIX)] * 9
                 + [pl.BlockSpec((None, D_MODEL, D_MODEL), lambda i: (layer, 0, 0)), vec(), vec()],
        out_specs=rows(D_MODEL), out_shape=jax.ShapeDtypeStruct((n, D_MODEL), F32),
        compiler_params=_cparams(("parallel",)),
    )(x, ya, yb, yc, *outs, *lses, w_out_bf, g, b)


def _ffn(x, w1_bf, w2_bf, g, b, layer):
    n = x.shape[0]
    tm = _tile_rows(n, 1024)
    tf = 1024

    def vec():
        return pl.BlockSpec((None, 1, D_MODEL), lambda i, j: (layer, 0, 0))

    return pl.pallas_call(
        _ffn_kernel, grid=(n // tm, D_FF // tf), name="ffn_ln",
        in_specs=[pl.BlockSpec((tm, D_MODEL), lambda i, j: (i, 0)),
                  pl.BlockSpec((None, D_MODEL, tf), lambda i, j: (layer, 0, j)),
                  pl.BlockSpec((None, tf, D_MODEL), lambda i, j: (layer, j, 0)), vec(), vec()],
        out_specs=pl.BlockSpec((tm, D_MODEL), lambda i, j: (i, 0)),
        out_shape=jax.ShapeDtypeStruct((n, D_MODEL), F32),
        scratch_shapes=[pltpu.VMEM((tm, D_MODEL), F32), pltpu.VMEM((tm, D_MODEL), BF16)],
        compiler_params=_cparams(("parallel", "arbitrary")),
    )(x, w1_bf, w2_bf, g, b)


def kernel(x, emb_ln_g, emb_ln_b, w_in, w_fnet, q_norm_g, k_norm_g, conv_dw, conv_b, conv_ln_g, conv_ln_b,
           w_conv_out, w_out, ln1_g, ln1_b, w_ff1, w_ff2, ln2_g, ln2_b, rel_bias):
    batch, seq, d = x.shape
    depth = w_in.shape[0]
    assert d == D_MODEL and seq % (FFT_R * 16) == 0 and seq % (DIL_CONFIGS[-1][1] * DIL_N) == 0
    n = batch * seq
    a = seq // FFT_R
    heads = W_MIX // HEAD_DIM
    kvw = N_KV_GQA * HEAD_DIM

    cos_np, sin_np = _rope_tables(seq)
    cos, sin = jnp.asarray(cos_np), jnp.asarray(sin_np)
    ones = jnp.asarray(np.kron(np.eye(heads), np.ones((HEAD_DIM, HEAD_DIM))), BF16)
    g1_np, g2_np = _fft_stage_mats(a)
    g1, g2 = jnp.asarray(g1_np, BF16), jnp.asarray(g2_np, BF16)
    twr_np, twi_np = _twiddle(a, seq)
    twr = jnp.repeat(jnp.asarray(twr_np), W_MIX, axis=1)
    twi = jnp.repeat(jnp.asarray(twi_np), W_MIX, axis=1)

    w_in_bf = w_in.astype(BF16)
    w_out_bf = w_out.astype(BF16)
    w_ff1_bf = w_ff1.astype(BF16)
    w_ff2_bf = w_ff2.astype(BF16)
    wpw_bf = w_conv_out.astype(BF16)
    gq = jnp.tile(q_norm_g, (1, heads)).reshape(depth, 1, W_MIX)
    gk = jnp.tile(k_norm_g, (1, N_KV_GQA)).reshape(depth, 1, kvw)
    r3 = lambda t: t.reshape(depth, 1, t.shape[-1])

    wfold = _fold_fourier_weights(w_in, w_fnet, seq)
    tq_d = min(128, seq // DIL_CONFIGS[-1][1])
    bias = _bias_tiles(rel_bias, tq_d)

    h = _embed_ln(x.reshape(n, d), emb_ln_g, emb_ln_b)
    for l in range(depth):
        zr, zi, qn, kn, v, uc, qd, kd, vd = _inproj(h, wfold, w_in_bf, cos, sin, gq, gk, ones, l, seq)
        ya = _fourier(zr, zi, g1, g2, twr, twi, batch, seq)
        yb = _gqa(qn, kn, v, batch, seq)
        yc = _conv(uc, conv_dw, r3(conv_b), r3(conv_ln_g), r3(conv_ln_b), wpw_bf, l, seq)
        outs, lses = [], []
        for br, (_, dil) in enumerate(DIL_CONFIGS):
            o, lse = _dilated_branch(qd, kd, vd, bias, br, dil, tq_d, batch, seq)
            outs.append(o)
            lses.append(lse)
        h = _outproj(h, ya, yb, yc, outs, lses, w_out_bf, r3(ln1_g), r3(ln1_b), l)
        h = _ffn(h, w_ff1_bf, w_ff2_bf, r3(ln2_g), r3(ln2_b), l)
    return h.reshape(batch, seq, d)
```

```python
import functools
import math

import numpy as np
import jax
import jax.numpy as jnp
from jax import lax
from jax.experimental import pallas as pl
from jax.experimental.pallas import tpu as pltpu

F32 = jnp.float32
BF16 = jnp.bfloat16

D_MODEL = 1024
DEPTH = 4
HEAD_DIM = 64
W_MIX = 256
N_KV_GQA = 2
CONV_WIDTH = 31
CONV_PAD = CONV_WIDTH // 2
DIL_CONFIGS = ((128, 1), (512, 4), (2048, 16))
DIL_N = 64
D_FF = 4 * D_MODEL
GRID_W = 64
ROPE_THETA = 10000.0
REL_BUCKETS = 32
REL_MAX_DIST = 1024
ALPHA = (2 * DEPTH) ** 0.25
LN_EPS = 1e-5
RMS_EPS = 1e-6
NEG = -1e30
ATTN_SCALE = HEAD_DIM ** -0.5
LOG2E = math.log2(math.e)

COL_QB, COL_KVB, COL_GLU, COL_QD, COL_KD, COL_VD, D_IN = 256, 512, 768, 1280, 1536, 1792, 2048

FFT_R = 64
HALO = 16
VMEM_LIMIT = 56 * 1024 * 1024


def _cparams(sem):
    return pltpu.CompilerParams(dimension_semantics=sem, vmem_limit_bytes=VMEM_LIMIT)


def _dft_cos_sin(n):
    j = np.arange(n)
    ang = 2.0 * np.pi * ((j[:, None] * j[None, :]) % n) / n
    return np.cos(ang), np.sin(ang)


def _channel_dft_blockdiag(seq):
    c, s = _dft_cos_sin(HEAD_DIM)
    scale = 1.0 / math.sqrt(seq * HEAD_DIM)
    eye = np.eye(W_MIX // HEAD_DIM)
    return (np.kron(eye, c) * scale).astype(np.float32), (np.kron(eye, -s) * scale).astype(np.float32)


def _fft_stage_mats(a):
    ca, sa = _dft_cos_sin(a)
    g1 = np.block([[ca, sa], [-sa, ca]])
    c2, s2 = _dft_cos_sin(FFT_R)
    g2 = np.concatenate([c2, s2], axis=1)
    return g1.astype(np.float32), g2.astype(np.float32)


def _twiddle(a, seq):
    c = np.arange(a)[:, None]
    b = np.arange(FFT_R)[None, :]
    ang = 2.0 * np.pi * ((b * c) % seq) / seq
    return np.cos(ang).astype(np.float32), (-np.sin(ang)).astype(np.float32)


def _rope_tables(seq):
    rows = seq // GRID_W
    row = np.repeat(np.arange(rows), GRID_W).astype(np.float32)
    col = np.tile(np.arange(GRID_W), rows).astype(np.float32)
    nf = HEAD_DIM // 4
    inv = (ROPE_THETA ** (-np.arange(nf, dtype=np.float32) / nf)).astype(np.float32)
    ar = row[:, None] * inv
    ac = col[:, None] * inv
    cos = np.concatenate([np.cos(ar), np.cos(ar), np.cos(ac), np.cos(ac)], -1)
    sin = np.concatenate([-np.sin(ar), np.sin(ar), -np.sin(ac), np.sin(ac)], -1)
    reps = W_MIX // HEAD_DIM
    return np.tile(cos, (1, reps)).astype(np.float32), np.tile(sin, (1, reps)).astype(np.float32)


def _t5_bucket_np(rel):
    nb = REL_BUCKETS // 2
    max_exact = nb // 2
    ret = np.where(rel > 0, nb, 0)
    n = np.abs(rel)
    nf = np.maximum(n, 1).astype(np.float32)
    large = max_exact + (np.log(nf / np.float32(max_exact)) / np.float32(math.log(REL_MAX_DIST / max_exact))
                         * np.float32(nb - max_exact)).astype(np.int32)
    large = np.minimum(large, nb - 1)
    return ret + np.where(n < max_exact, n, large)


def _bucket_tiles(tq):
    qi = np.arange(tq)[:, None]
    ki = np.arange(tq + 2 * DIL_N)[None, :]
    rel = ki - DIL_N - qi
    tiles = []
    for _, dil in DIL_CONFIGS:
        tiles.append(np.where(np.abs(rel) <= DIL_N, _t5_bucket_np(rel * dil), -1))
    return np.stack(tiles).astype(np.int32)


def _layer_norm(r, g, b):
    mu = jnp.mean(r, axis=-1, keepdims=True)
    d = r - mu
    var = jnp.mean(d * d, axis=-1, keepdims=True)
    return d * lax.rsqrt(var + LN_EPS) * g + b


def _dot(a, b):
    return jnp.dot(a, b, preferred_element_type=F32)


def _dot_nt(a, b):
    return lax.dot_general(a, b, (((1,), (1,)), ((), ())), preferred_element_type=F32)


def _norm_rope(x, g, cos, sin, ones):
    x2 = x * x
    hi = x2.astype(BF16)
    lo = (x2 - hi.astype(F32)).astype(BF16)
    ss = _dot(hi, ones) + _dot(lo, ones)
    xn = x * lax.rsqrt(ss * (1.0 / HEAD_DIM) + RMS_EPS) * g
    w = x.shape[1]
    lane = lax.broadcasted_iota(jnp.int32, x.shape, 1)
    quarter = HEAD_DIM // 4
    partner = jnp.where((lane % (2 * quarter)) < quarter,
                        pltpu.roll(xn, w - quarter, 1), pltpu.roll(xn, quarter, 1))
    return xn * cos + partner * sin


def _embed_ln_kernel(x_ref, g_ref, b_ref, o_ref):
    o_ref[...] = _layer_norm(x_ref[...], g_ref[...], b_ref[...])


def _fold_kernel(win_ref, wf_ref, c_ref, s_ref, o_ref):
    hp = lax.Precision.HIGHEST
    wf = wf_ref[...]
    mc = jnp.dot(c_ref[...], wf, precision=hp, preferred_element_type=F32)
    ms = jnp.dot(s_ref[...], wf, precision=hp, preferred_element_type=F32)
    win = win_ref[...]
    o_ref[:, :W_MIX] = jnp.dot(win, mc, precision=hp, preferred_element_type=F32).astype(BF16)
    o_ref[:, W_MIX:] = jnp.dot(win, ms, precision=hp, preferred_element_type=F32).astype(BF16)


def _inproj_kernel(x_ref, wfold_ref, w_ref, wvt_ref, cos_ref, sin_ref, gq_ref, gk_ref, ones_ref,
                   zr_ref, zi_ref, qn_ref, kn_ref, vt_ref, uc_ref, qd_ref, kd_ref, vd_ref):
    xb = x_ref[...].astype(BF16)
    z = _dot(xb, wfold_ref[...])
    zr_ref[...] = z[:, :W_MIX].astype(BF16)
    zi_ref[...] = z[:, W_MIX:].astype(BF16)
    cos = cos_ref[...]
    sin = sin_ref[...]
    ones = ones_ref[...]
    q = _dot(xb, w_ref[:, COL_QB:COL_KVB])
    qn_ref[...] = (_norm_rope(q, gq_ref[...], cos, sin, ones) * (ATTN_SCALE * LOG2E)).astype(BF16)
    kvw = N_KV_GQA * HEAD_DIM
    k = _dot(xb, w_ref[:, COL_KVB:COL_KVB + kvw])
    kn_ref[...] = _norm_rope(k, gk_ref[...], cos[:, :kvw], sin[:, :kvw], ones[:kvw, :kvw]).astype(BF16)
    vt = _dot_nt(wvt_ref[...], xb)
    row = lax.broadcasted_iota(jnp.int32, (HEAD_DIM, vt.shape[1]), 0)
    one_row = jnp.where(row == 0, 1.0, 0.0).astype(BF16)
    vt_ref[...] = jnp.concatenate(
        [piece for g in range(N_KV_GQA)
         for piece in (vt[g * HEAD_DIM:(g + 1) * HEAD_DIM].astype(BF16), one_row)], axis=0)
    uc_ref[...] = _dot(xb, w_ref[:, COL_GLU:COL_QD])
    qd_ref[...] = _dot(xb, w_ref[:, COL_QD:COL_KD]).astype(BF16)
    kd_ref[...] = _dot(xb, w_ref[:, COL_KD:COL_VD]).astype(BF16)
    vd_ref[...] = _dot(xb, w_ref[:, COL_VD:D_IN]).astype(BF16)


def _fft1_kernel(zr_ref, zi_ref, g1_ref, twr_ref, twi_ref, yr_ref, yi_ref, *, a, nb):
    z = jnp.concatenate([zr_ref[...], zi_ref[...]], axis=0)
    y = _dot(g1_ref[...], z)
    yr, yi = y[:a], y[a:]
    twr, twi = twr_ref[...], twi_ref[...]
    pr = yr * twr - yi * twi
    pi = yr * twi + yi * twr
    for j in range(nb):
        yr_ref[j] = pr[:, j * W_MIX:(j + 1) * W_MIX].astype(BF16)
        yi_ref[j] = pi[:, j * W_MIX:(j + 1) * W_MIX].astype(BF16)


def _fft2_kernel(yr_ref, yi_ref, g2_ref, o_ref):
    y = jnp.concatenate([yr_ref[...], yi_ref[...]], axis=0)
    o_ref[...] = _dot(g2_ref[...], y).astype(BF16)


def _gqa_kernel(q_ref, k_ref, vt_ref, o_ref, st_a, st_b, *, tq, tk, seq):
    nk = seq // tk
    rep = (W_MIX // HEAD_DIM) // N_KV_GQA
    zeros = jnp.zeros((tq, HEAD_DIM), BF16)
    q2 = []
    for g in range(N_KV_GQA):
        parts = []
        for r in range(rep):
            h = g * rep + r
            qh = q_ref[:, h * HEAD_DIM:(h + 1) * HEAD_DIM]
            parts.append(jnp.concatenate([zeros] * g + [qh] + [zeros] * (N_KV_GQA - 1 - g), axis=1))
        q2.append(jnp.concatenate(parts, axis=0))
    rows = rep * tq
    vw = 2 * HEAD_DIM

    def scores(c, st_scr):
        off = pl.multiple_of(c * tk, tk)
        kc = k_ref[pl.ds(off, tk), :]
        for g in range(N_KV_GQA):
            st_scr[g] = _dot_nt(kc, q2[g])

    def softmax_pv(c, st_scr, carry):
        off = pl.multiple_of(c * tk, tk)
        out = []
        for g in range(N_KV_GQA):
            m, acc = carry[g]
            st = st_scr[g]
            vt = vt_ref[g * vw:(g + 1) * vw, pl.ds(off, tk)]
            m_new = jnp.maximum(m, jnp.max(st, axis=0, keepdims=True))
            alpha = jnp.exp2(m - m_new)
            pt = jnp.exp2(st - m_new).astype(BF16)
            out.append((m_new, alpha * acc + _dot(vt, pt)))
        return tuple(out)

    def body(c2, carry):
        c = 2 * c2
        scores(c + 1, st_b)
        carry = softmax_pv(c, st_a, carry)
        scores((c + 2) % nk, st_a)
        return softmax_pv(c + 1, st_b, carry)

    init = tuple((jnp.full((1, rows), -jnp.inf, F32), jnp.zeros((vw, rows), F32)) for _ in range(N_KV_GQA))
    scores(0, st_a)
    final = lax.fori_loop(0, nk // 2, body, init)
    for g in range(N_KV_GQA):
        acc = final[g][1]
        o = (acc[:HEAD_DIM] / acc[HEAD_DIM:HEAD_DIM + 1]).T
        for r in range(rep):
            h = g * rep + r
            o_ref[:, h * HEAD_DIM:(h + 1) * HEAD_DIM] = o[r * tq:(r + 1) * tq].astype(BF16)


def _conv_kernel(cur_ref, prev_ref, next_ref, dw_ref, cb_ref, g_ref, b_ref, wpw_ref, o_ref, hpad, *, ts, nblk):
    c = pl.program_id(0) % nblk

    def glu(u):
        return u[:, :W_MIX] * jax.nn.sigmoid(u[:, W_MIX:])

    hpad[0:HALO, :] = jnp.where(c > 0, glu(prev_ref[...]), 0.0)
    hpad[HALO:HALO + ts, :] = glu(cur_ref[...])
    hpad[HALO + ts:HALO + ts + HALO, :] = jnp.where(c < nblk - 1, glu(next_ref[...]), 0.0)
    acc = jnp.zeros((ts, W_MIX), F32)
    for j in range(CONV_WIDTH):
        start = HALO - CONV_PAD + j
        acc = acc + dw_ref[j:j + 1, :] * hpad[start:start + ts, :]
    h = _layer_norm(acc + cb_ref[...], g_ref[...], b_ref[...])
    h = h * jax.nn.sigmoid(h)
    o_ref[...] = _dot(h.astype(BF16), wpw_ref[...]).astype(BF16)


def _bias_kernel(idx_ref, rb_ref, o_ref):
    idx = idx_ref[...]
    for h in range(W_MIX // HEAD_DIM):
        tile = jnp.full(idx.shape, NEG, F32)
        for b in range(REL_BUCKETS):
            tile = jnp.where(idx == b, rb_ref[b, h], tile)
        o_ref[h] = tile


def _dilated_kernel(q_ref, k_ref, v_ref, bias_ref, o_ref, lse_ref, kpad, vpad, *, length, tq):
    n = DIL_N
    win = tq + 2 * n
    zeros = jnp.zeros((n, W_MIX), BF16)
    kpad[0:n, :] = zeros
    vpad[0:n, :] = zeros
    kpad[n + length:n + length + n, :] = zeros
    vpad[n + length:n + length + n, :] = zeros
    kpad[n:n + length, :] = k_ref[...]
    vpad[n:n + length, :] = v_ref[...]

    def body(i, carry):
        m0 = pl.multiple_of(i * tq, tq)
        qb = q_ref[pl.ds(m0, tq), :]
        kw = kpad[pl.ds(m0, win), :]
        vw = vpad[pl.ds(m0, win), :]
        kpos = m0 - n + lax.broadcasted_iota(jnp.int32, (tq, win), 1)
        valid = (kpos >= 0) & (kpos < length)
        for h in range(W_MIX // HEAD_DIM):
            sl = slice(h * HEAD_DIM, (h + 1) * HEAD_DIM)
            s = _dot_nt(qb[:, sl], kw[:, sl]) * ATTN_SCALE + bias_ref[h]
            s = jnp.where(valid, s, NEG)
            m = jnp.max(s, axis=-1, keepdims=True)
            p = jnp.exp(s - m)
            den = jnp.sum(p, axis=-1, keepdims=True)
            o = _dot(p.astype(BF16), vw[:, sl]) / den
            lse = m + jnp.log(den)
            o_ref[pl.ds(m0, tq), sl] = o
            lse_ref[pl.ds(m0, tq), sl] = jnp.broadcast_to(lse, (tq, HEAD_DIM))
        return carry

    nblk = length // tq
    lax.fori_loop(0, nblk, body, 0, unroll=2 if nblk % 2 == 0 else 1)


def _outproj_kernel(x_ref, ya_ref, yb_ref, yc_ref, o1_ref, o2_ref, o3_ref, l1_ref, l2_ref, l3_ref,
                    w_ref, g_ref, b_ref, out_ref):
    l1, l2, l3 = l1_ref[...], l2_ref[...], l3_ref[...]
    m = jnp.maximum(jnp.maximum(l1, l2), l3)
    e1, e2, e3 = jnp.exp(l1 - m), jnp.exp(l2 - m), jnp.exp(l3 - m)
    yd = (e1 * o1_ref[...] + e2 * o2_ref[...] + e3 * o3_ref[...]) / (e1 + e2 + e3)
    y = _dot(ya_ref[...], w_ref[0:W_MIX, :])
    y = y + _dot(yb_ref[...], w_ref[W_MIX:2 * W_MIX, :])
    y = y + _dot(yc_ref[...], w_ref[2 * W_MIX:3 * W_MIX, :])
    y = y + _dot(yd.astype(BF16), w_ref[3 * W_MIX:4 * W_MIX, :])
    out_ref[...] = _layer_norm(ALPHA * x_ref[...] + y, g_ref[...], b_ref[...])


def _ffn_kernel(x_ref, w1_ref, w2_ref, g_ref, b_ref, o_ref, acc_ref, xb_ref):
    j = pl.program_id(1)

    @pl.when(j == 0)
    def _():
        acc_ref[...] = jnp.zeros_like(acc_ref)
        xb_ref[...] = x_ref[...].astype(BF16)

    h = _dot(xb_ref[...], w1_ref[...])
    h = jnp.square(jnp.maximum(h, 0.0)).astype(BF16)
    acc_ref[...] += _dot(h, w2_ref[...])

    @pl.when(j == pl.num_programs(1) - 1)
    def _():
        o_ref[...] = _layer_norm(ALPHA * x_ref[...] + acc_ref[...], g_ref[...], b_ref[...])


def _tile_rows(n, want):
    t = min(n, want)
    assert n % t == 0, (n, t)
    return t


def _embed_ln(x, g, b):
    n, d = x.shape
    tm = _tile_rows(n, 512)
    return pl.pallas_call(
        _embed_ln_kernel, grid=(n // tm,), name="embed_ln",
        in_specs=[pl.BlockSpec((tm, d), lambda i: (i, 0)), pl.BlockSpec((1, d), lambda i: (0, 0)),
                  pl.BlockSpec((1, d), lambda i: (0, 0))],
        out_specs=pl.BlockSpec((tm, d), lambda i: (i, 0)),
        out_shape=jax.ShapeDtypeStruct((n, d), F32), compiler_params=_cparams(("parallel",)),
    )(x, g.reshape(1, d), b.reshape(1, d))


def _fold_fourier_weights(w_in, w_fnet, seq):
    c_bd, s_bd = _channel_dft_blockdiag(seq)
    depth = w_in.shape[0]
    sq = pl.BlockSpec((W_MIX, W_MIX), lambda l: (0, 0))
    return pl.pallas_call(
        _fold_kernel, grid=(depth,), name="fold_fourier_weights",
        in_specs=[pl.BlockSpec((None, D_MODEL, W_MIX), lambda l: (l, 0, 0)),
                  pl.BlockSpec((None, W_MIX, W_MIX), lambda l: (l, 0, 0)), sq, sq],
        out_specs=pl.BlockSpec((None, D_MODEL, 2 * W_MIX), lambda l: (l, 0, 0)),
        out_shape=jax.ShapeDtypeStruct((depth, D_MODEL, 2 * W_MIX), BF16),
        compiler_params=_cparams(("parallel",)),
    )(w_in, w_fnet, jnp.asarray(c_bd), jnp.asarray(s_bd))


def _inproj(x, wfold, w_in_bf, wvt_bf, cos, sin, gq, gk, ones, layer, seq):
    n = x.shape[0]
    tm = _tile_rows(seq, 512)
    kvw = N_KV_GQA * HEAD_DIM
    pos_blocks = seq // tm

    def rows(c):
        return pl.BlockSpec((tm, c), lambda i: (i, 0))

    def whole(r, c):
        return pl.BlockSpec((r, c), lambda i: (0, 0))

    out_cols = (W_MIX, W_MIX, W_MIX, kvw, None, 2 * W_MIX, W_MIX, W_MIX, W_MIX)
    out_dtypes = (BF16, BF16, BF16, BF16, BF16, F32, BF16, BF16, BF16)
    vt_spec = pl.BlockSpec((2 * kvw, tm), lambda i: (0, i))
    vt_shape = jax.ShapeDtypeStruct((2 * kvw, n), BF16)
    return pl.pallas_call(
        _inproj_kernel, grid=(n // tm,), name="inproj",
        in_specs=[rows(D_MODEL),
                  pl.BlockSpec((None, D_MODEL, 2 * W_MIX), lambda i: (layer, 0, 0)),
                  pl.BlockSpec((None, D_MODEL, D_IN), lambda i: (layer, 0, 0)),
                  pl.BlockSpec((None, kvw, D_MODEL), lambda i: (layer, 0, 0)),
                  pl.BlockSpec((tm, W_MIX), lambda i: (i % pos_blocks, 0)),
                  pl.BlockSpec((tm, W_MIX), lambda i: (i % pos_blocks, 0)),
                  pl.BlockSpec((None, 1, W_MIX), lambda i: (layer, 0, 0)),
                  pl.BlockSpec((None, 1, kvw), lambda i: (layer, 0, 0)),
                  whole(W_MIX, W_MIX)],
        out_specs=[vt_spec if c is None else rows(c) for c in out_cols],
        out_shape=[vt_shape if c is None else jax.ShapeDtypeStruct((n, c), dt) for c, dt in zip(out_cols, out_dtypes)],
        compiler_params=_cparams(("parallel",)),
    )(x, wfold, w_in_bf, wvt_bf, cos, sin, gq, gk, ones)


def _fourier(zr, zi, g1, g2, twr, twi, batch, seq):
    a = seq // FFT_R
    lanes = FFT_R * W_MIX
    nb = 16
    zr3 = zr.reshape(batch, a, lanes)
    zi3 = zi.reshape(batch, a, lanes)
    blk = pl.BlockSpec((None, a, nb * W_MIX), lambda b, j: (b, 0, j))
    tw = pl.BlockSpec((a, nb * W_MIX), lambda b, j: (0, j))
    y_spec = pl.BlockSpec((None, nb, a, W_MIX), lambda b, j: (b, j, 0, 0))
    y_shape = jax.ShapeDtypeStruct((batch, FFT_R, a, W_MIX), BF16)
    yr, yi = pl.pallas_call(
        functools.partial(_fft1_kernel, a=a, nb=nb), grid=(batch, FFT_R // nb), name="fourier_stage1",
        in_specs=[blk, blk, pl.BlockSpec((2 * a, 2 * a), lambda b, j: (0, 0)), tw, tw],
        out_specs=[y_spec, y_spec], out_shape=[y_shape, y_shape],
        compiler_params=_cparams(("parallel", "parallel")),
    )(zr3, zi3, g1, twr, twi)
    nc = min(a, 16)
    yr2 = yr.reshape(batch, FFT_R, a * W_MIX)
    yi2 = yi.reshape(batch, FFT_R, a * W_MIX)
    blk2 = pl.BlockSpec((None, FFT_R, nc * W_MIX), lambda b, j: (b, 0, j))
    out = pl.pallas_call(
        _fft2_kernel, grid=(batch, a // nc), name="fourier_stage2",
        in_specs=[blk2, blk2, pl.BlockSpec((FFT_R, 2 * FFT_R), lambda b, j: (0, 0))],
        out_specs=blk2, out_shape=jax.ShapeDtypeStruct((batch, FFT_R, a * W_MIX), BF16),
        compiler_params=_cparams(("parallel", "parallel")),
    )(yr2, yi2, g2)
    return out.reshape(batch * seq, W_MIX)


def _gqa(qn, kn, v, batch, seq):
    tq = _tile_rows(seq, 256)
    tk = _tile_rows(seq, 512)
    nq = seq // tq
    kvw = N_KV_GQA * HEAD_DIM
    rep = (W_MIX // HEAD_DIM) // N_KV_GQA
    assert (seq // tk) % 2 == 0
    return pl.pallas_call(
        functools.partial(_gqa_kernel, tq=tq, tk=tk, seq=seq), grid=(batch, nq), name="gqa_attention",
        in_specs=[pl.BlockSpec((tq, W_MIX), lambda b, i: (b * nq + i, 0)),
                  pl.BlockSpec((seq, kvw), lambda b, i: (b, 0)),
                  pl.BlockSpec((2 * kvw, seq), lambda b, i: (0, b))],
        out_specs=pl.BlockSpec((tq, W_MIX), lambda b, i: (b * nq + i, 0)),
        out_shape=jax.ShapeDtypeStruct((batch * seq, W_MIX), BF16),
        scratch_shapes=[pltpu.VMEM((N_KV_GQA, tk, rep * tq), F32)] * 2,
        compiler_params=_cparams(("parallel", "parallel")),
    )(qn, kn, v)


def _conv(uc, dw, cb, g, b, wpw_bf, layer, seq):
    n = uc.shape[0]
    ts = _tile_rows(seq, 512)
    nblk = seq // ts
    hb = ts // HALO
    last = n // HALO - 1

    def vec():
        return pl.BlockSpec((None, 1, W_MIX), lambda i: (layer, 0, 0))

    return pl.pallas_call(
        functools.partial(_conv_kernel, ts=ts, nblk=nblk), grid=(n // ts,), name="conformer_conv",
        in_specs=[pl.BlockSpec((ts, 2 * W_MIX), lambda i: (i, 0)),
                  pl.BlockSpec((HALO, 2 * W_MIX), lambda i: (jnp.maximum(i * hb - 1, 0), 0)),
                  pl.BlockSpec((HALO, 2 * W_MIX), lambda i: (jnp.minimum((i + 1) * hb, last), 0)),
                  pl.BlockSpec((None, CONV_WIDTH, W_MIX), lambda i: (layer, 0, 0)),
                  vec(), vec(), vec(),
                  pl.BlockSpec((None, W_MIX, W_MIX), lambda i: (layer, 0, 0))],
        out_specs=pl.BlockSpec((ts, W_MIX), lambda i: (i, 0)),
        out_shape=jax.ShapeDtypeStruct((n, W_MIX), BF16),
        scratch_shapes=[pltpu.VMEM((ts + 2 * HALO, W_MIX), F32)],
        compiler_params=_cparams(("parallel",)),
    )(uc, uc, uc, dw, cb, g, b, wpw_bf)


def _bias_tiles(rel_bias, tq):
    idx = jnp.asarray(_bucket_tiles(tq))
    nbr, _, win = idx.shape
    heads = W_MIX // HEAD_DIM
    return pl.pallas_call(
        _bias_kernel, grid=(nbr,), name="dilated_bias_tiles",
        in_specs=[pl.BlockSpec((None, tq, win), lambda r: (r, 0, 0)),
                  pl.BlockSpec(memory_space=pltpu.SMEM)],
        out_specs=pl.BlockSpec((None, heads, tq, win), lambda r: (r, 0, 0, 0)),
        out_shape=jax.ShapeDtypeStruct((nbr, heads, tq, win), F32),
        compiler_params=_cparams(("parallel",)),
    )(idx, rel_bias)


def _dilated_branch(qd, kd, vd, bias, branch, dil, tq, batch, seq):
    length = seq // dil
    heads = W_MIX // HEAD_DIM
    win = tq + 2 * DIL_N

    def view(t):
        return t.reshape(batch, length, dil * W_MIX)

    blk = pl.BlockSpec((None, length, W_MIX), lambda b, r: (b, 0, r))
    o_shape = jax.ShapeDtypeStruct((batch, length, dil * W_MIX), F32)
    o, lse = pl.pallas_call(
        functools.partial(_dilated_kernel, length=length, tq=tq), grid=(batch, dil), name=f"dilated_d{dil}",
        in_specs=[blk, blk, blk, pl.BlockSpec((None, heads, tq, win), lambda b, r: (branch, 0, 0, 0))],
        out_specs=[blk, blk], out_shape=[o_shape, o_shape],
        scratch_shapes=[pltpu.VMEM((length + 2 * DIL_N, W_MIX), BF16), pltpu.VMEM((length + 2 * DIL_N, W_MIX), BF16)],
        compiler_params=_cparams(("parallel", "parallel")),
    )(view(qd), view(kd), view(vd), bias)
    return o.reshape(batch * seq, W_MIX), lse.reshape(batch * seq, W_MIX)


def _outproj(x, ya, yb, yc, outs, lses, w_out_bf, g, b, layer):
    n = x.shape[0]
    tm = _tile_rows(n, 512)

    def rows(c):
        return pl.BlockSpec((tm, c), lambda i: (i, 0))

    def vec():
        return pl.BlockSpec((None, 1, D_MODEL), lambda i: (layer, 0, 0))

    return pl.pallas_call(
        _outproj_kernel, grid=(n // tm,), name="outproj_ln",
        in_specs=[rows(D_MODEL)] + [rows(W_MIX)] * 9
                 + [pl.BlockSpec((None, D_MODEL, D_MODEL), lambda i: (layer, 0, 0)), vec(), vec()],
        out_specs=rows(D_MODEL), out_shape=jax.ShapeDtypeStruct((n, D_MODEL), F32),
        compiler_params=_cparams(("parallel",)),
    )(x, ya, yb, yc, *outs, *lses, w_out_bf, g, b)


def _ffn(x, w1_bf, w2_bf, g, b, layer):
    n = x.shape[0]
    tm = _tile_rows(n, 1024)
    tf = 1024

    def vec():
        return pl.BlockSpec((None, 1, D_MODEL), lambda i, j: (layer, 0, 0))

    return pl.pallas_call(
        _ffn_kernel, grid=(n // tm, D_FF // tf), name="ffn_ln",
        in_specs=[pl.BlockSpec((tm, D_MODEL), lambda i, j: (i, 0)),
                  pl.BlockSpec((None, D_MODEL, tf), lambda i, j: (layer, 0, j)),
                  pl.BlockSpec((None, tf, D_MODEL), lambda i, j: (layer, j, 0)), vec(), vec()],
        out_specs=pl.BlockSpec((tm, D_MODEL), lambda i, j: (i, 0)),
        out_shape=jax.ShapeDtypeStruct((n, D_MODEL), F32),
        scratch_shapes=[pltpu.VMEM((tm, D_MODEL), F32), pltpu.VMEM((tm, D_MODEL), BF16)],
        compiler_params=_cparams(("parallel", "arbitrary")),
    )(x, w1_bf, w2_bf, g, b)


def kernel(x, emb_ln_g, emb_ln_b, w_in, w_fnet, q_norm_g, k_norm_g, conv_dw, conv_b, conv_ln_g, conv_ln_b,
           w_conv_out, w_out, ln1_g, ln1_b, w_ff1, w_ff2, ln2_g, ln2_b, rel_bias):
    batch, seq, d = x.shape
    depth = w_in.shape[0]
    assert d == D_MODEL and seq % (FFT_R * 16) == 0 and seq % (DIL_CONFIGS[-1][1] * DIL_N) == 0
    n = batch * seq
    a = seq // FFT_R
    heads = W_MIX // HEAD_DIM
    kvw = N_KV_GQA * HEAD_DIM

    cos_np, sin_np = _rope_tables(seq)
    cos, sin = jnp.asarray(cos_np), jnp.asarray(sin_np)
    ones = jnp.asarray(np.kron(np.eye(heads), np.ones((HEAD_DIM, HEAD_DIM))), BF16)
    g1_np, g2_np = _fft_stage_mats(a)
    g1, g2 = jnp.asarray(g1_np, BF16), jnp.asarray(g2_np, BF16)
    twr_np, twi_np = _twiddle(a, seq)
    twr = jnp.repeat(jnp.asarray(twr_np), W_MIX, axis=1)
    twi = jnp.repeat(jnp.asarray(twi_np), W_MIX, axis=1)

    w_in_bf = w_in.astype(BF16)
    wvt_bf = jnp.swapaxes(w_in_bf[:, :, COL_KVB + kvw:COL_GLU], 1, 2)
    w_out_bf = w_out.astype(BF16)
    w_ff1_bf = w_ff1.astype(BF16)
    w_ff2_bf = w_ff2.astype(BF16)
    wpw_bf = w_conv_out.astype(BF16)
    gq = jnp.tile(q_norm_g, (1, heads)).reshape(depth, 1, W_MIX)
    gk = jnp.tile(k_norm_g, (1, N_KV_GQA)).reshape(depth, 1, kvw)
    r3 = lambda t: t.reshape(depth, 1, t.shape[-1])

    wfold = _fold_fourier_weights(w_in, w_fnet, seq)
    tq_d = min(128, seq // DIL_CONFIGS[-1][1])
    bias = _bias_tiles(rel_bias, tq_d)

    h = _embed_ln(x.reshape(n, d), emb_ln_g, emb_ln_b)
    for l in range(depth):
        zr, zi, qn, kn, v, uc, qd, kd, vd = _inproj(h, wfold, w_in_bf, wvt_bf, cos, sin, gq, gk, ones, l, seq)
        ya = _fourier(zr, zi, g1, g2, twr, twi, batch, seq)
        yb = _gqa(qn, kn, v, batch, seq)
        yc = _conv(uc, conv_dw, r3(conv_b), r3(conv_ln_g), r3(conv_ln_b), wpw_bf, l, seq)
        outs, lses = [], []
        for br, (_, dil) in enumerate(DIL_CONFIGS):
            o, lse = _dilated_branch(qd, kd, vd, bias, br, dil, tq_d, batch, seq)
            outs.append(o)
            lses.append(lse)
        h = _outproj(h, ya, yb, yc, outs, lses, w_out_bf, r3(ln1_g), r3(ln1_b), l)
        h = _ffn(h, w_ff1_bf, w_ff2_bf, r3(ln2_g), r3(ln2_b), l)
    return h.reshape(batch, seq, d)
```

```python
import functools
import math

import numpy as np
import jax
import jax.numpy as jnp
from jax import lax
from jax.experimental import pallas as pl
from jax.experimental.pallas import tpu as pltpu

F32 = jnp.float32
BF16 = jnp.bfloat16

D_MODEL = 1024
DEPTH = 4
HEAD_DIM = 64
W_MIX = 256
N_KV_GQA = 2
CONV_WIDTH = 31
CONV_PAD = CONV_WIDTH // 2
DIL_CONFIGS = ((128, 1), (512, 4), (2048, 16))
DIL_N = 64
D_FF = 4 * D_MODEL
GRID_W = 64
ROPE_THETA = 10000.0
REL_BUCKETS = 32
REL_MAX_DIST = 1024
ALPHA = (2 * DEPTH) ** 0.25
LN_EPS = 1e-5
RMS_EPS = 1e-6
NEG = -1e30
ATTN_SCALE = HEAD_DIM ** -0.5
LOG2E = math.log2(math.e)

COL_QB, COL_KVB, COL_GLU, COL_QD, COL_KD, COL_VD, D_IN = 256, 512, 768, 1280, 1536, 1792, 2048

FFT_R = 64
INPROJ_ROWS = 1024
HALO = 16
VMEM_LIMIT = 56 * 1024 * 1024


def _cparams(sem):
    return pltpu.CompilerParams(dimension_semantics=sem, vmem_limit_bytes=VMEM_LIMIT)


def _dft_cos_sin(n):
    j = np.arange(n)
    ang = 2.0 * np.pi * ((j[:, None] * j[None, :]) % n) / n
    return np.cos(ang), np.sin(ang)


def _channel_dft_blockdiag(seq):
    c, s = _dft_cos_sin(HEAD_DIM)
    scale = 1.0 / math.sqrt(seq * HEAD_DIM)
    eye = np.eye(W_MIX // HEAD_DIM)
    return (np.kron(eye, c) * scale).astype(np.float32), (np.kron(eye, -s) * scale).astype(np.float32)


def _fft_stage_mats(a):
    ca, sa = _dft_cos_sin(a)
    g1 = np.block([[ca, sa], [-sa, ca]])
    c2, s2 = _dft_cos_sin(FFT_R)
    g2 = np.concatenate([c2, s2], axis=1)
    return g1.astype(np.float32), g2.astype(np.float32)


def _twiddle(a, seq):
    c = np.arange(a)[:, None]
    b = np.arange(FFT_R)[None, :]
    ang = 2.0 * np.pi * ((b * c) % seq) / seq
    return np.cos(ang).astype(np.float32), (-np.sin(ang)).astype(np.float32)


def _rope_tables(seq):
    rows = seq // GRID_W
    row = np.repeat(np.arange(rows), GRID_W).astype(np.float32)
    col = np.tile(np.arange(GRID_W), rows).astype(np.float32)
    nf = HEAD_DIM // 4
    inv = (ROPE_THETA ** (-np.arange(nf, dtype=np.float32) / nf)).astype(np.float32)
    ar = row[:, None] * inv
    ac = col[:, None] * inv
    cos = np.concatenate([np.cos(ar), np.cos(ar), np.cos(ac), np.cos(ac)], -1)
    sin = np.concatenate([-np.sin(ar), np.sin(ar), -np.sin(ac), np.sin(ac)], -1)
    reps = W_MIX // HEAD_DIM
    return np.tile(cos, (1, reps)).astype(np.float32), np.tile(sin, (1, reps)).astype(np.float32)


def _t5_bucket_np(rel):
    nb = REL_BUCKETS // 2
    max_exact = nb // 2
    ret = np.where(rel > 0, nb, 0)
    n = np.abs(rel)
    nf = np.maximum(n, 1).astype(np.float32)
    large = max_exact + (np.log(nf / np.float32(max_exact)) / np.float32(math.log(REL_MAX_DIST / max_exact))
                         * np.float32(nb - max_exact)).astype(np.int32)
    large = np.minimum(large, nb - 1)
    return ret + np.where(n < max_exact, n, large)


def _bucket_tiles(tq):
    qi = np.arange(tq)[:, None]
    ki = np.arange(tq + 2 * DIL_N)[None, :]
    rel = ki - DIL_N - qi
    tiles = []
    for _, dil in DIL_CONFIGS:
        tiles.append(np.where(np.abs(rel) <= DIL_N, _t5_bucket_np(rel * dil), -1))
    return np.stack(tiles).astype(np.int32)


def _layer_norm(r, g, b):
    mu = jnp.mean(r, axis=-1, keepdims=True)
    d = r - mu
    var = jnp.mean(d * d, axis=-1, keepdims=True)
    return d * lax.rsqrt(var + LN_EPS) * g + b


def _dot(a, b):
    return jnp.dot(a, b, preferred_element_type=F32)


def _dot_nt(a, b):
    return lax.dot_general(a, b, (((1,), (1,)), ((), ())), preferred_element_type=F32)


def _norm_rope(x, g, cos, sin, ones):
    x2 = x * x
    hi = x2.astype(BF16)
    lo = (x2 - hi.astype(F32)).astype(BF16)
    ss = _dot(hi, ones) + _dot(lo, ones)
    xn = x * lax.rsqrt(ss * (1.0 / HEAD_DIM) + RMS_EPS) * g
    w = x.shape[1]
    lane = lax.broadcasted_iota(jnp.int32, x.shape, 1)
    quarter = HEAD_DIM // 4
    partner = jnp.where((lane % (2 * quarter)) < quarter,
                        pltpu.roll(xn, w - quarter, 1), pltpu.roll(xn, quarter, 1))
    return xn * cos + partner * sin


def _embed_ln_kernel(x_ref, g_ref, b_ref, o_ref):
    o_ref[...] = _layer_norm(x_ref[...], g_ref[...], b_ref[...])


def _fold_kernel(win_ref, wf_ref, c_ref, s_ref, o_ref):
    hp = lax.Precision.HIGHEST
    wf = wf_ref[...]
    mc = jnp.dot(c_ref[...], wf, precision=hp, preferred_element_type=F32)
    ms = jnp.dot(s_ref[...], wf, precision=hp, preferred_element_type=F32)
    win = win_ref[...]
    o_ref[:, :W_MIX] = jnp.dot(win, mc, precision=hp, preferred_element_type=F32).astype(BF16)
    o_ref[:, W_MIX:] = jnp.dot(win, ms, precision=hp, preferred_element_type=F32).astype(BF16)


def _inproj_kernel(x_ref, wfold_ref, w_ref, wvt_ref, cos_ref, sin_ref, gq_ref, gk_ref, ones_ref,
                   zr_ref, zi_ref, qn_ref, kn_ref, vt_ref, uc_ref, qd_ref, kd_ref, vd_ref, stage_q, stage_k, stage_v):
    xb = x_ref[...].astype(BF16)
    z = _dot(xb, wfold_ref[...])
    zr_ref[...] = z[:, :W_MIX]
    zi_ref[...] = z[:, W_MIX:]
    cos = cos_ref[...]
    sin = sin_ref[...]
    ones = ones_ref[...]
    q = _dot(xb, w_ref[:, COL_QB:COL_KVB])
    qn_ref[...] = (_norm_rope(q, gq_ref[...], cos, sin, ones) * (ATTN_SCALE * LOG2E)).astype(BF16)
    kvw = N_KV_GQA * HEAD_DIM
    k = _dot(xb, w_ref[:, COL_KVB:COL_KVB + kvw])
    kn_ref[...] = _norm_rope(k, gk_ref[...], cos[:, :kvw], sin[:, :kvw], ones[:kvw, :kvw]).astype(BF16)
    vt = _dot_nt(wvt_ref[...], xb)
    row = lax.broadcasted_iota(jnp.int32, (HEAD_DIM, vt.shape[1]), 0)
    one_row = jnp.where(row == 0, 1.0, 0.0).astype(BF16)
    vt_ref[...] = jnp.concatenate(
        [piece for g in range(N_KV_GQA)
         for piece in (vt[g * HEAD_DIM:(g + 1) * HEAD_DIM].astype(BF16), one_row)], axis=0)
    uc_ref[...] = _dot(xb, w_ref[:, COL_GLU:COL_QD])
    tm = xb.shape[0]
    for col, scale, out_ref, stage in ((COL_QD, ATTN_SCALE * LOG2E, qd_ref, stage_q), (COL_KD, None, kd_ref, stage_k),
                                       (COL_VD, None, vd_ref, stage_v)):
        u = _dot(xb, w_ref[:, col:col + W_MIX])
        if scale is not None:
            u = u * scale
        out_ref[0] = u.astype(BF16)
        halves = W_MIX // 128
        for half in range(halves):
            stage[half] = u[:, half * 128:(half + 1) * 128]
        for br, (_, dil) in enumerate(DIL_CONFIGS):
            if dil == 1:
                continue
            c = tm // dil
            for r in range(dil):
                out_ref[br, r * c:(r + 1) * c, :] = jnp.concatenate(
                    [stage[half, pl.ds(r, c, stride=dil), :] for half in range(halves)], axis=1).astype(BF16)


def _fourier_kernel(zr_ref, zi_ref, g1_ref, g2_ref, twr_ref, twi_ref, o_ref, yr_scr, yi_scr, *, a):
    g1 = g1_ref[...]
    g2 = g2_ref[...]

    def stage1(b, carry):
        z = jnp.concatenate([zr_ref[pl.ds(b, a, stride=FFT_R), :], zi_ref[pl.ds(b, a, stride=FFT_R), :]], axis=0)
        y = _dot(g1, z.astype(BF16))
        yr, yi = y[:a], y[a:]
        twr = twr_ref[b]
        twi = twi_ref[b]
        row0 = pl.multiple_of(b * a, a)
        yr_scr[pl.ds(row0, a), :] = yr * twr - yi * twi
        yi_scr[pl.ds(row0, a), :] = yr * twi + yi * twr
        return carry

    lax.fori_loop(0, FFT_R, stage1, 0, unroll=4)

    def stage2(c, carry):
        y = jnp.concatenate([yr_scr[pl.ds(c, FFT_R, stride=a), :], yi_scr[pl.ds(c, FFT_R, stride=a), :]], axis=0)
        o_ref[pl.ds(c, FFT_R, stride=a), :] = _dot(g2, y.astype(BF16))
        return carry

    lax.fori_loop(0, a, stage2, 0, unroll=4)


def _gqa_kernel(q_ref, k_ref, vt_ref, o_ref, st_a, st_b, *, tq, tk, seq):
    nk = seq // tk
    rep = (W_MIX // HEAD_DIM) // N_KV_GQA
    zeros = jnp.zeros((tq, HEAD_DIM), BF16)
    q2 = []
    for g in range(N_KV_GQA):
        parts = []
        for r in range(rep):
            h = g * rep + r
            qh = q_ref[:, h * HEAD_DIM:(h + 1) * HEAD_DIM]
            parts.append(jnp.concatenate([zeros] * g + [qh] + [zeros] * (N_KV_GQA - 1 - g), axis=1))
        q2.append(jnp.concatenate(parts, axis=0))
    rows = rep * tq
    vw = 2 * HEAD_DIM

    def scores(c, st_scr):
        off = pl.multiple_of(c * tk, tk)
        kc = k_ref[pl.ds(off, tk), :]
        for g in range(N_KV_GQA):
            st_scr[g] = _dot_nt(kc, q2[g])

    def softmax_pv(c, st_scr, carry):
        off = pl.multiple_of(c * tk, tk)
        out = []
        for g in range(N_KV_GQA):
            m, acc = carry[g]
            st = st_scr[g]
            vt = vt_ref[g * vw:(g + 1) * vw, pl.ds(off, tk)]
            m_new = jnp.maximum(m, jnp.max(st, axis=0, keepdims=True))
            alpha = jnp.exp2(m - m_new)
            pt = jnp.exp2(st - m_new).astype(BF16)
            out.append((m_new, alpha * acc + _dot(vt, pt)))
        return tuple(out)

    def body(c2, carry):
        c = 2 * c2
        scores(c + 1, st_b)
        carry = softmax_pv(c, st_a, carry)
        scores((c + 2) % nk, st_a)
        return softmax_pv(c + 1, st_b, carry)

    init = tuple((jnp.full((1, rows), -jnp.inf, F32), jnp.zeros((vw, rows), F32)) for _ in range(N_KV_GQA))
    scores(0, st_a)
    final = lax.fori_loop(0, nk // 2, body, init)
    for g in range(N_KV_GQA):
        acc = final[g][1]
        o = (acc[:HEAD_DIM] / acc[HEAD_DIM:HEAD_DIM + 1]).T
        for r in range(rep):
            h = g * rep + r
            o_ref[:, h * HEAD_DIM:(h + 1) * HEAD_DIM] = o[r * tq:(r + 1) * tq].astype(BF16)


def _conv_kernel(cur_ref, prev_ref, next_ref, dw_ref, cb_ref, g_ref, b_ref, wpw_ref, o_ref, hpad, *, ts, nblk):
    c = pl.program_id(0) % nblk

    def glu(u):
        return u[:, :W_MIX] * jax.nn.sigmoid(u[:, W_MIX:])

    hpad[0:HALO, :] = jnp.where(c > 0, glu(prev_ref[...]), 0.0)
    hpad[HALO:HALO + ts, :] = glu(cur_ref[...])
    hpad[HALO + ts:HALO + ts + HALO, :] = jnp.where(c < nblk - 1, glu(next_ref[...]), 0.0)
    acc = jnp.zeros((ts, W_MIX), F32)
    for j in range(CONV_WIDTH):
        start = HALO - CONV_PAD + j
        acc = acc + dw_ref[j:j + 1, :] * hpad[start:start + ts, :]
    h = _layer_norm(acc + cb_ref[...], g_ref[...], b_ref[...])
    h = h * jax.nn.sigmoid(h)
    o_ref[...] = _dot(h.astype(BF16), wpw_ref[...]).astype(BF16)


def _bias_kernel(idx_ref, rb_ref, o_ref):
    idx = idx_ref[...]
    for h in range(W_MIX // HEAD_DIM):
        tile = jnp.full(idx.shape, NEG, F32)
        for b in range(REL_BUCKETS):
            tile = jnp.where(idx == b, rb_ref[b, h] * LOG2E, tile)
        o_ref[h] = tile


def _dilated_branch(q_in, k_in, v_in, bias_ref, qbuf, kbuf, vbuf, m_st, acc_st, s_a, s_b, *, dil, first, seq, tm, tq):
    n = DIL_N
    heads = W_MIX // HEAD_DIM
    length = seq // dil
    lp = length + 2 * n
    c = tm // dil
    win = tq + 2 * n
    nb = length // tq
    nblocks = dil * nb
    slab = 2 * HEAD_DIM

    lane = lax.broadcasted_iota(jnp.int32, (c, HEAD_DIM), 1)
    one_col = jnp.where(lane == 0, 1.0, 0.0).astype(BF16)
    for r in range(dil):
        base = r * lp
        for buf in (kbuf, vbuf):
            zeros = jnp.zeros((n, buf.shape[1]), BF16)
            buf[base:base + n, :] = zeros
            buf[base + n + length:base + lp, :] = zeros
        for t in range(seq // tm):
            src = slice(r * c, (r + 1) * c)
            qbuf[r * length + t * c:r * length + (t + 1) * c, :] = q_in[t, src, :]
            kbuf[base + n + t * c:base + n + (t + 1) * c, :] = k_in[t, src, :]
            vv = v_in[t, src, :]
            vbuf[base + n + t * c:base + n + (t + 1) * c, :] = jnp.concatenate(
                [piece for h in range(heads) for piece in (vv[:, h * HEAD_DIM:(h + 1) * HEAD_DIM], one_col)], axis=1)

    head_of_lane = lax.broadcasted_iota(jnp.int32, (tq, W_MIX), 1) // HEAD_DIM

    def block_pos(j):
        r = j // nb
        i = j - r * nb
        return r, i, pl.multiple_of(j * tq + r * 2 * n, 2 * n if tq % (2 * n) == 0 else tq)

    def scores(j, s_scr):
        _, _, krow = block_pos(j)
        qb = qbuf[pl.ds(pl.multiple_of(j * tq, tq), tq), :]
        kw = kbuf[pl.ds(krow, win), :]
        for h in range(heads):
            s_scr[h] = _dot_nt(jnp.where(head_of_lane == h, qb, jnp.zeros_like(qb)), kw)

    def update(j, s_scr):
        r, i, krow = block_pos(j)
        vw = vbuf[pl.ds(krow, win), :]
        kpos = i * tq - n + lax.broadcasted_iota(jnp.int32, (tq, win), 1)
        valid = (kpos >= 0) & (kpos < length)
        rows = pl.ds(r + dil * i * tq, tq, stride=dil) if dil > 1 else pl.ds(pl.multiple_of(i * tq, tq), tq)
        for h in range(heads):
            s = jnp.where(valid, s_scr[h] + bias_ref[h], NEG)
            mblk = jnp.max(s, axis=-1, keepdims=True)
            if first:
                m_new = jnp.broadcast_to(mblk, (tq, slab))
            else:
                m_old = m_st[h, rows, :]
                m_new = jnp.maximum(m_old, mblk)
            m_row = jnp.tile(m_new, (1, win // slab)) if win % slab == 0 else m_new[:, :1]
            p = jnp.exp2(s - m_row).astype(BF16)
            pv = _dot(p, vw[:, h * slab:(h + 1) * slab])
            if not first:
                pv = jnp.exp2(m_old - m_new) * acc_st[h, rows, :] + pv
            m_st[h, rows, :] = m_new
            acc_st[h, rows, :] = pv

    def body(j2, carry):
        j = 2 * j2
        scores(j + 1, s_b)
        update(j, s_a)
        scores((j + 2) % nblocks, s_a)
        update(j + 1, s_b)
        return carry

    scores(0, s_a)
    lax.fori_loop(0, nblocks // 2, body, 0)


def _dilated_kernel(q_in, k_in, v_in, bias_ref, o_ref, qbuf, kbuf, vbuf, m_st, acc_st, s_a, s_b, *, seq, tm, tq):
    br = pl.program_id(1)
    for idx, (_, dil) in enumerate(DIL_CONFIGS):
        @pl.when(br == idx)
        def _(idx=idx, dil=dil):
            _dilated_branch(q_in, k_in, v_in, bias_ref, qbuf, kbuf, vbuf, m_st, acc_st, s_a, s_b,
                            dil=dil, first=(idx == 0), seq=seq, tm=tm, tq=tq)

    @pl.when(br == len(DIL_CONFIGS) - 1)
    def _():
        heads = W_MIX // HEAD_DIM
        rows_per = 256

        def finish(t, carry):
            rows = pl.ds(pl.multiple_of(t * rows_per, rows_per), rows_per)
            outs = []
            for h in range(heads):
                a = acc_st[h, rows, :]
                outs.append(a[:, :HEAD_DIM] / a[:, HEAD_DIM:HEAD_DIM + 1])
            o_ref[rows, :] = jnp.concatenate(outs, axis=1).astype(BF16)
            return carry

        lax.fori_loop(0, seq // rows_per, finish, 0)


def _outproj_kernel(x_ref, ya_ref, yb_ref, yc_ref, yd_ref, w_ref, g_ref, b_ref, out_ref):
    y = _dot(ya_ref[...].astype(BF16), w_ref[0:W_MIX, :])
    y = y + _dot(yb_ref[...], w_ref[W_MIX:2 * W_MIX, :])
    y = y + _dot(yc_ref[...], w_ref[2 * W_MIX:3 * W_MIX, :])
    y = y + _dot(yd_ref[...], w_ref[3 * W_MIX:4 * W_MIX, :])
    out_ref[...] = _layer_norm(ALPHA * x_ref[...] + y, g_ref[...], b_ref[...])


def _ffn_kernel(x_ref, w1_ref, w2_ref, g_ref, b_ref, o_ref, acc_ref, xb_ref):
    j = pl.program_id(1)

    @pl.when(j == 0)
    def _():
        acc_ref[...] = jnp.zeros_like(acc_ref)
        xb_ref[...] = x_ref[...].astype(BF16)

    h = _dot(xb_ref[...], w1_ref[...])
    h = jnp.square(jnp.maximum(h, 0.0)).astype(BF16)
    acc_ref[...] += _dot(h, w2_ref[...])

    @pl.when(j == pl.num_programs(1) - 1)
    def _():
        o_ref[...] = _layer_norm(ALPHA * x_ref[...] + acc_ref[...], g_ref[...], b_ref[...])


def _tile_rows(n, want):
    t = min(n, want)
    assert n % t == 0, (n, t)
    return t


def _embed_ln(x, g, b):
    n, d = x.shape
    tm = _tile_rows(n, 512)
    return pl.pallas_call(
        _embed_ln_kernel, grid=(n // tm,), name="embed_ln",
        in_specs=[pl.BlockSpec((tm, d), lambda i: (i, 0)), pl.BlockSpec((1, d), lambda i: (0, 0)),
                  pl.BlockSpec((1, d), lambda i: (0, 0))],
        out_specs=pl.BlockSpec((tm, d), lambda i: (i, 0)),
        out_shape=jax.ShapeDtypeStruct((n, d), F32), compiler_params=_cparams(("parallel",)),
    )(x, g.reshape(1, d), b.reshape(1, d))


def _fold_fourier_weights(w_in, w_fnet, seq):
    c_bd, s_bd = _channel_dft_blockdiag(seq)
    depth = w_in.shape[0]
    sq = pl.BlockSpec((W_MIX, W_MIX), lambda l: (0, 0))
    return pl.pallas_call(
        _fold_kernel, grid=(depth,), name="fold_fourier_weights",
        in_specs=[pl.BlockSpec((None, D_MODEL, W_MIX), lambda l: (l, 0, 0)),
                  pl.BlockSpec((None, W_MIX, W_MIX), lambda l: (l, 0, 0)), sq, sq],
        out_specs=pl.BlockSpec((None, D_MODEL, 2 * W_MIX), lambda l: (l, 0, 0)),
        out_shape=jax.ShapeDtypeStruct((depth, D_MODEL, 2 * W_MIX), BF16),
        compiler_params=_cparams(("parallel",)),
    )(w_in, w_fnet, jnp.asarray(c_bd), jnp.asarray(s_bd))


def _inproj(x, wfold, w_in_bf, wvt_bf, cos, sin, gq, gk, ones, layer, batch, seq):
    n = x.shape[0]
    tm = _tile_rows(seq, INPROJ_ROWS)
    kvw = N_KV_GQA * HEAD_DIM
    pos_blocks = seq // tm
    nbr = len(DIL_CONFIGS)
    assert DIL_CONFIGS[0][1] == 1 and all(tm % (dil * 16) == 0 for _, dil in DIL_CONFIGS)

    def rows(c):
        return pl.BlockSpec((tm, c), lambda i: (i, 0))

    def whole(r, c):
        return pl.BlockSpec((r, c), lambda i: (0, 0))

    def flat(c, dt):
        return rows(c), jax.ShapeDtypeStruct((n, c), dt)

    dil_out = (pl.BlockSpec((None, None, nbr, tm, W_MIX), lambda i: (i // pos_blocks, i % pos_blocks, 0, 0, 0)),
               jax.ShapeDtypeStruct((batch, pos_blocks, nbr, tm, W_MIX), BF16))
    vt_out = (pl.BlockSpec((2 * kvw, tm), lambda i: (0, i)),
              jax.ShapeDtypeStruct((2 * kvw, n), BF16))
    outs = [flat(W_MIX, F32), flat(W_MIX, F32), flat(W_MIX, BF16), flat(kvw, BF16), vt_out, flat(2 * W_MIX, F32),
            dil_out, dil_out, dil_out]
    return pl.pallas_call(
        _inproj_kernel, grid=(n // tm,), name="inproj",
        in_specs=[rows(D_MODEL),
                  pl.BlockSpec((None, D_MODEL, 2 * W_MIX), lambda i: (layer, 0, 0)),
                  pl.BlockSpec((None, D_MODEL, D_IN), lambda i: (layer, 0, 0)),
                  pl.BlockSpec((None, kvw, D_MODEL), lambda i: (layer, 0, 0)),
                  pl.BlockSpec((tm, W_MIX), lambda i: (i % pos_blocks, 0)),
                  pl.BlockSpec((tm, W_MIX), lambda i: (i % pos_blocks, 0)),
                  pl.BlockSpec((None, 1, W_MIX), lambda i: (layer, 0, 0)),
                  pl.BlockSpec((None, 1, kvw), lambda i: (layer, 0, 0)),
                  whole(W_MIX, W_MIX)],
        out_specs=[o[0] for o in outs], out_shape=[o[1] for o in outs],
        scratch_shapes=[pltpu.VMEM((W_MIX // 128, tm, 128), F32)] * 3,
        compiler_params=_cparams(("parallel",)),
    )(x, wfold, w_in_bf, wvt_bf, cos, sin, gq, gk, ones)


def _fourier(zr, zi, g1, g2, twr, twi, batch, seq):
    a = seq // FFT_R
    blk = pl.BlockSpec((seq, 128), lambda b, j: (b, j))

    def whole(shape):
        return pl.BlockSpec(shape, lambda b, j: (0,) * len(shape))

    return pl.pallas_call(
        functools.partial(_fourier_kernel, a=a), grid=(batch, W_MIX // 128), name="fourier_mix",
        in_specs=[blk, blk, whole((2 * a, 2 * a)), whole((FFT_R, 2 * FFT_R)), whole((FFT_R, a, 128)),
                  whole((FFT_R, a, 128))],
        out_specs=blk, out_shape=jax.ShapeDtypeStruct((batch * seq, W_MIX), F32),
        scratch_shapes=[pltpu.VMEM((seq, 128), F32)] * 2,
        compiler_params=_cparams(("parallel", "parallel")),
    )(zr, zi, g1, g2, twr, twi)


def _gqa(qn, kn, v, batch, seq):
    tq = _tile_rows(seq, 256)
    tk = _tile_rows(seq, 512)
    nq = seq // tq
    kvw = N_KV_GQA * HEAD_DIM
    rep = (W_MIX // HEAD_DIM) // N_KV_GQA
    assert (seq // tk) % 2 == 0
    return pl.pallas_call(
        functools.partial(_gqa_kernel, tq=tq, tk=tk, seq=seq), grid=(batch, nq), name="gqa_attention",
        in_specs=[pl.BlockSpec((tq, W_MIX), lambda b, i: (b * nq + i, 0)),
                  pl.BlockSpec((seq, kvw), lambda b, i: (b, 0)),
                  pl.BlockSpec((2 * kvw, seq), lambda b, i: (0, b))],
        out_specs=pl.BlockSpec((tq, W_MIX), lambda b, i: (b * nq + i, 0)),
        out_shape=jax.ShapeDtypeStruct((batch * seq, W_MIX), BF16),
        scratch_shapes=[pltpu.VMEM((N_KV_GQA, tk, rep * tq), F32)] * 2,
        compiler_params=_cparams(("parallel", "parallel")),
    )(qn, kn, v)


def _conv(uc, dw, cb, g, b, wpw_bf, layer, seq):
    n = uc.shape[0]
    ts = _tile_rows(seq, 512)
    nblk = seq // ts
    hb = ts // HALO
    last = n // HALO - 1

    def vec():
        return pl.BlockSpec((None, 1, W_MIX), lambda i: (layer, 0, 0))

    return pl.pallas_call(
        functools.partial(_conv_kernel, ts=ts, nblk=nblk), grid=(n // ts,), name="conformer_conv",
        in_specs=[pl.BlockSpec((ts, 2 * W_MIX), lambda i: (i, 0)),
                  pl.BlockSpec((HALO, 2 * W_MIX), lambda i: (jnp.maximum(i * hb - 1, 0), 0)),
                  pl.BlockSpec((HALO, 2 * W_MIX), lambda i: (jnp.minimum((i + 1) * hb, last), 0)),
                  pl.BlockSpec((None, CONV_WIDTH, W_MIX), lambda i: (layer, 0, 0)),
                  vec(), vec(), vec(),
                  pl.BlockSpec((None, W_MIX, W_MIX), lambda i: (layer, 0, 0))],
        out_specs=pl.BlockSpec((ts, W_MIX), lambda i: (i, 0)),
        out_shape=jax.ShapeDtypeStruct((n, W_MIX), BF16),
        scratch_shapes=[pltpu.VMEM((ts + 2 * HALO, W_MIX), F32)],
        compiler_params=_cparams(("parallel",)),
    )(uc, uc, uc, dw, cb, g, b, wpw_bf)


def _bias_tiles(rel_bias, tq):
    idx = jnp.asarray(_bucket_tiles(tq))
    nbr, _, win = idx.shape
    heads = W_MIX // HEAD_DIM
    return pl.pallas_call(
        _bias_kernel, grid=(nbr,), name="dilated_bias_tiles",
        in_specs=[pl.BlockSpec((None, tq, win), lambda r: (r, 0, 0)),
                  pl.BlockSpec(memory_space=pltpu.SMEM)],
        out_specs=pl.BlockSpec((None, heads, tq, win), lambda r: (r, 0, 0, 0)),
        out_shape=jax.ShapeDtypeStruct((nbr, heads, tq, win), F32),
        compiler_params=_cparams(("parallel",)),
    )(idx, rel_bias)


def _dilated(qd, kd, vd, bias, tq, batch, seq):
    heads = W_MIX // HEAD_DIM
    win = tq + 2 * DIL_N
    tiles, nbr, tm = qd.shape[1], qd.shape[2], qd.shape[3]
    max_dil = max(dil for _, dil in DIL_CONFIGS)
    pad_rows = seq + max_dil * 2 * DIL_N
    assert (seq // tq) % 2 == 0
    blk = pl.BlockSpec((None, tiles, None, tm, W_MIX), lambda b, r: (b, 0, r, 0, 0))
    return pl.pallas_call(
        functools.partial(_dilated_kernel, seq=seq, tm=tm, tq=tq), grid=(batch, nbr), name="dilated_attention",
        in_specs=[blk, blk, blk, pl.BlockSpec((None, heads, tq, win), lambda b, r: (r, 0, 0, 0))],
        out_specs=pl.BlockSpec((seq, W_MIX), lambda b, r: (b, 0)),
        out_shape=jax.ShapeDtypeStruct((batch * seq, W_MIX), BF16),
        scratch_shapes=[pltpu.VMEM((seq, W_MIX), BF16), pltpu.VMEM((pad_rows, W_MIX), BF16),
                        pltpu.VMEM((pad_rows, 2 * W_MIX), BF16),
                        pltpu.VMEM((heads, seq, 2 * HEAD_DIM), F32), pltpu.VMEM((heads, seq, 2 * HEAD_DIM), F32),
                        pltpu.VMEM((heads, tq, win), F32), pltpu.VMEM((heads, tq, win), F32)],
        compiler_params=_cparams(("parallel", "arbitrary")),
    )(qd, kd, vd, bias)


def _outproj(x, ya, yb, yc, yd, w_out_bf, g, b, layer):
    n = x.shape[0]
    tm = _tile_rows(n, 512)

    def rows(c):
        return pl.BlockSpec((tm, c), lambda i: (i, 0))

    def vec():
        return pl.BlockSpec((None, 1, D_MODEL), lambda i: (layer, 0, 0))

    return pl.pallas_call(
        _outproj_kernel, grid=(n // tm,), name="outproj_ln",
        in_specs=[rows(D_MODEL)] + [rows(W_MIX)] * 4
                 + [pl.BlockSpec((None, D_MODEL, D_MODEL), lambda i: (layer, 0, 0)), vec(), vec()],
        out_specs=rows(D_MODEL), out_shape=jax.ShapeDtypeStruct((n, D_MODEL), F32),
        compiler_params=_cparams(("parallel",)),
    )(x, ya, yb, yc, yd, w_out_bf, g, b)


def _ffn(x, w1_bf, w2_bf, g, b, layer):
    n = x.shape[0]
    tm = _tile_rows(n, 1024)
    tf = 1024

    def vec():
        return pl.BlockSpec((None, 1, D_MODEL), lambda i, j: (layer, 0, 0))

    return pl.pallas_call(
        _ffn_kernel, grid=(n // tm, D_FF // tf), name="ffn_ln",
        in_specs=[pl.BlockSpec((tm, D_MODEL), lambda i, j: (i, 0)),
                  pl.BlockSpec((None, D_MODEL, tf), lambda i, j: (layer, 0, j)),
                  pl.BlockSpec((None, tf, D_MODEL), lambda i, j: (layer, j, 0)), vec(), vec()],
        out_specs=pl.BlockSpec((tm, D_MODEL), lambda i, j: (i, 0)),
        out_shape=jax.ShapeDtypeStruct((n, D_MODEL), F32),
        scratch_shapes=[pltpu.VMEM((tm, D_MODEL), F32), pltpu.VMEM((tm, D_MODEL), BF16)],
        compiler_params=_cparams(("parallel", "arbitrary")),
    )(x, w1_bf, w2_bf, g, b)


def kernel(x, emb_ln_g, emb_ln_b, w_in, w_fnet, q_norm_g, k_norm_g, conv_dw, conv_b, conv_ln_g, conv_ln_b,
           w_conv_out, w_out, ln1_g, ln1_b, w_ff1, w_ff2, ln2_g, ln2_b, rel_bias):
    batch, seq, d = x.shape
    depth = w_in.shape[0]
    assert d == D_MODEL and seq % (FFT_R * 16) == 0 and seq % (DIL_CONFIGS[-1][1] * DIL_N) == 0
    n = batch * seq
    a = seq // FFT_R
    heads = W_MIX // HEAD_DIM
    kvw = N_KV_GQA * HEAD_DIM

    cos_np, sin_np = _rope_tables(seq)
    cos, sin = jnp.asarray(cos_np), jnp.asarray(sin_np)
    ones = jnp.asarray(np.kron(np.eye(heads), np.ones((HEAD_DIM, HEAD_DIM))), BF16)
    g1_np, g2_np = _fft_stage_mats(a)
    g1, g2 = jnp.asarray(g1_np, BF16), jnp.asarray(g2_np, BF16)
    twr_np, twi_np = _twiddle(a, seq)
    twr = jnp.broadcast_to(jnp.asarray(twr_np.T)[:, :, None], (FFT_R, a, 128))
    twi = jnp.broadcast_to(jnp.asarray(twi_np.T)[:, :, None], (FFT_R, a, 128))

    w_in_bf = w_in.astype(BF16)
    wvt_bf = jnp.swapaxes(w_in_bf[:, :, COL_KVB + kvw:COL_GLU], 1, 2)
    w_out_bf = w_out.astype(BF16)
    w_ff1_bf = w_ff1.astype(BF16)
    w_ff2_bf = w_ff2.astype(BF16)
    wpw_bf = w_conv_out.astype(BF16)
    gq = jnp.tile(q_norm_g, (1, heads)).reshape(depth, 1, W_MIX)
    gk = jnp.tile(k_norm_g, (1, N_KV_GQA)).reshape(depth, 1, kvw)
    r3 = lambda t: t.reshape(depth, 1, t.shape[-1])

    wfold = _fold_fourier_weights(w_in, w_fnet, seq)
    tq_d = min(128, seq // DIL_CONFIGS[-1][1])
    bias = _bias_tiles(rel_bias, tq_d)

    h = _embed_ln(x.reshape(n, d), emb_ln_g, emb_ln_b)
    for l in range(depth):
        zr, zi, qn, kn, v, uc, qd, kd, vd = _inproj(h, wfold, w_in_bf, wvt_bf, cos, sin, gq, gk, ones, l, batch, seq)
        ya = _fourier(zr, zi, g1, g2, twr, twi, batch, seq)
        yb = _gqa(qn, kn, v, batch, seq)
        yc = _conv(uc, conv_dw, r3(conv_b), r3(conv_ln_g), r3(conv_ln_b), wpw_bf, l, seq)
        yd = _dilated(qd, kd, vd, bias, tq_d, batch, seq)
        h = _outproj(h, ya, yb, yc, yd, w_out_bf, r3(ln1_g), r3(ln1_b), l)
        h = _ffn(h, w_ff1_bf, w_ff2_bf, r3(ln2_g), r3(ln2_b), l)
    return h.reshape(batch, seq, d)
```

```python
import functools
import math

import numpy as np
import jax
import jax.numpy as jnp
from jax import lax
from jax.experimental import pallas as pl
from jax.experimental.pallas import tpu as pltpu

F32 = jnp.float32
BF16 = jnp.bfloat16

D_MODEL = 1024
DEPTH = 4
HEAD_DIM = 64
W_MIX = 256
N_KV_GQA = 2
CONV_WIDTH = 31
CONV_PAD = CONV_WIDTH // 2
DIL_CONFIGS = ((128, 1), (512, 4), (2048, 16))
DIL_N = 64
D_FF = 4 * D_MODEL
GRID_W = 64
ROPE_THETA = 10000.0
REL_BUCKETS = 32
REL_MAX_DIST = 1024
ALPHA = (2 * DEPTH) ** 0.25
LN_EPS = 1e-5
RMS_EPS = 1e-6
NEG = -1e30
ATTN_SCALE = HEAD_DIM ** -0.5
LOG2E = math.log2(math.e)

COL_QB, COL_KVB, COL_GLU, COL_QD, COL_KD, COL_VD, D_IN = 256, 512, 768, 1280, 1536, 1792, 2048

FFT_R = 64
INPROJ_ROWS = 1024
GQA_UNIT_LANES = 256
FFN_CHUNK = 2048
FFN_SUB_ROWS = 256
HALO = 16
VMEM_LIMIT = 56 * 1024 * 1024


def _cparams(sem):
    return pltpu.CompilerParams(dimension_semantics=sem, vmem_limit_bytes=VMEM_LIMIT)


def _dft_cos_sin(n):
    j = np.arange(n)
    ang = 2.0 * np.pi * ((j[:, None] * j[None, :]) % n) / n
    return np.cos(ang), np.sin(ang)


def _channel_dft_blockdiag(seq):
    c, s = _dft_cos_sin(HEAD_DIM)
    scale = 1.0 / math.sqrt(seq * HEAD_DIM)
    eye = np.eye(W_MIX // HEAD_DIM)
    return (np.kron(eye, c) * scale).astype(np.float32), (np.kron(eye, -s) * scale).astype(np.float32)


def _fft_stage_mats(a):
    ca, sa = _dft_cos_sin(a)
    g1 = np.block([[ca, sa], [-sa, ca]])
    c2, s2 = _dft_cos_sin(FFT_R)
    g2 = np.concatenate([c2, s2], axis=1)
    return g1.astype(np.float32), g2.astype(np.float32)


def _twiddle(a, seq):
    c = np.arange(a)[:, None]
    b = np.arange(FFT_R)[None, :]
    ang = 2.0 * np.pi * ((b * c) % seq) / seq
    return np.cos(ang).astype(np.float32), (-np.sin(ang)).astype(np.float32)


def _rope_tables(seq):
    rows = seq // GRID_W
    row = np.repeat(np.arange(rows), GRID_W).astype(np.float32)
    col = np.tile(np.arange(GRID_W), rows).astype(np.float32)
    nf = HEAD_DIM // 4
    inv = (ROPE_THETA ** (-np.arange(nf, dtype=np.float32) / nf)).astype(np.float32)
    ar = row[:, None] * inv
    ac = col[:, None] * inv
    cos = np.concatenate([np.cos(ar), np.cos(ar), np.cos(ac), np.cos(ac)], -1)
    sin = np.concatenate([-np.sin(ar), np.sin(ar), -np.sin(ac), np.sin(ac)], -1)
    reps = W_MIX // HEAD_DIM
    return np.tile(cos, (1, reps)).astype(np.float32), np.tile(sin, (1, reps)).astype(np.float32)


def _t5_bucket_np(rel):
    nb = REL_BUCKETS // 2
    max_exact = nb // 2
    ret = np.where(rel > 0, nb, 0)
    n = np.abs(rel)
    nf = np.maximum(n, 1).astype(np.float32)
    large = max_exact + (np.log(nf / np.float32(max_exact)) / np.float32(math.log(REL_MAX_DIST / max_exact))
                         * np.float32(nb - max_exact)).astype(np.int32)
    large = np.minimum(large, nb - 1)
    return ret + np.where(n < max_exact, n, large)


def _bucket_tiles(tq):
    qi = np.arange(tq)[:, None]
    ki = np.arange(tq + 2 * DIL_N)[None, :]
    rel = ki - DIL_N - qi
    tiles = []
    for _, dil in DIL_CONFIGS:
        tiles.append(np.where(np.abs(rel) <= DIL_N, _t5_bucket_np(rel * dil), -1))
    return np.stack(tiles).astype(np.int32)


def _layer_norm(r, g, b):
    mu = jnp.mean(r, axis=-1, keepdims=True)
    d = r - mu
    var = jnp.mean(d * d, axis=-1, keepdims=True)
    return d * lax.rsqrt(var + LN_EPS) * g + b


def _dot(a, b):
    return jnp.dot(a, b, preferred_element_type=F32)


def _dot_nt(a, b):
    return lax.dot_general(a, b, (((1,), (1,)), ((), ())), preferred_element_type=F32)


def _norm_rope(x, g, cos, sin, ones):
    x2 = x * x
    hi = x2.astype(BF16)
    lo = (x2 - hi.astype(F32)).astype(BF16)
    ss = _dot(hi, ones) + _dot(lo, ones)
    xn = x * lax.rsqrt(ss * (1.0 / HEAD_DIM) + RMS_EPS) * g
    w = x.shape[1]
    lane = lax.broadcasted_iota(jnp.int32, x.shape, 1)
    quarter = HEAD_DIM // 4
    partner = jnp.where((lane % (2 * quarter)) < quarter,
                        pltpu.roll(xn, w - quarter, 1), pltpu.roll(xn, quarter, 1))
    return xn * cos + partner * sin


def _embed_ln_kernel(x_ref, g_ref, b_ref, o_ref):
    o_ref[...] = _layer_norm(x_ref[...], g_ref[...], b_ref[...])


def _fold_kernel(win_ref, wf_ref, c_ref, s_ref, o_ref):
    hp = lax.Precision.HIGHEST
    wf = wf_ref[...]
    mc = jnp.dot(c_ref[...], wf, precision=hp, preferred_element_type=F32)
    ms = jnp.dot(s_ref[...], wf, precision=hp, preferred_element_type=F32)
    win = win_ref[...]
    o_ref[:, :W_MIX] = jnp.dot(win, mc, precision=hp, preferred_element_type=F32).astype(BF16)
    o_ref[:, W_MIX:] = jnp.dot(win, ms, precision=hp, preferred_element_type=F32).astype(BF16)


def _inproj_kernel(x_ref, wfold_ref, w_ref, wvt_ref, cos_ref, sin_ref, gq_ref, gk_ref, ones_ref,
                   zr_ref, zi_ref, qn_ref, kn_ref, vt_ref, uc_ref, qd_ref, kd_ref, vd_ref, stage_q, stage_k, stage_v):
    xb = x_ref[...].astype(BF16)
    kvw = N_KV_GQA * HEAD_DIM
    tm = xb.shape[0]
    q = _dot(xb, w_ref[:, COL_QB:COL_KVB])
    k = _dot(xb, w_ref[:, COL_KVB:COL_KVB + kvw])
    z = _dot(xb, wfold_ref[...])
    vt = _dot_nt(wvt_ref[...], xb)
    dil_u = [_dot(xb, w_ref[:, col:col + W_MIX]) for col in (COL_QD, COL_KD, COL_VD)]
    uc_ref[...] = _dot(xb, w_ref[:, COL_GLU:COL_QD])

    zr_ref[...] = z[:, :W_MIX]
    zi_ref[...] = z[:, W_MIX:]
    row = lax.broadcasted_iota(jnp.int32, (HEAD_DIM, tm), 0)
    one_row = jnp.where(row == 0, 1.0, 0.0).astype(BF16)
    vt_ref[...] = jnp.concatenate(
        [piece for g in range(N_KV_GQA)
         for piece in (vt[g * HEAD_DIM:(g + 1) * HEAD_DIM].astype(BF16), one_row)], axis=0)

    halves = W_MIX // 128
    for u, scale, out_ref, stage in zip(dil_u, (ATTN_SCALE * LOG2E, None, None), (qd_ref, kd_ref, vd_ref),
                                        (stage_q, stage_k, stage_v)):
        if scale is not None:
            u = u * scale
        out_ref[0] = u.astype(BF16)
        for half in range(halves):
            stage[half] = u[:, half * 128:(half + 1) * 128]
        for br, (_, dil) in enumerate(DIL_CONFIGS):
            if dil == 1:
                continue
            c = tm // dil
            for r in range(dil):
                out_ref[br, r * c:(r + 1) * c, :] = jnp.concatenate(
                    [stage[half, pl.ds(r, c, stride=dil), :] for half in range(halves)], axis=1).astype(BF16)

    cos = cos_ref[...]
    sin = sin_ref[...]
    ones = ones_ref[...]
    qn_ref[...] = (_norm_rope(q, gq_ref[...], cos, sin, ones) * (ATTN_SCALE * LOG2E)).astype(BF16)
    kn_ref[...] = _norm_rope(k, gk_ref[...], cos[:, :kvw], sin[:, :kvw], ones[:kvw, :kvw]).astype(BF16)


def _fourier_kernel(zr_ref, zi_ref, g1_ref, g2_ref, twr_ref, twi_ref, o_ref, yr_scr, yi_scr, *, a):
    g1 = g1_ref[...]
    g2 = g2_ref[...]

    def stage1(b, carry):
        z = jnp.concatenate([zr_ref[pl.ds(b, a, stride=FFT_R), :], zi_ref[pl.ds(b, a, stride=FFT_R), :]], axis=0)
        y = _dot(g1, z.astype(BF16))
        yr, yi = y[:a], y[a:]
        twr = twr_ref[b]
        twi = twi_ref[b]
        row0 = pl.multiple_of(b * a, a)
        yr_scr[pl.ds(row0, a), :] = yr * twr - yi * twi
        yi_scr[pl.ds(row0, a), :] = yr * twi + yi * twr
        return carry

    lax.fori_loop(0, FFT_R, stage1, 0, unroll=8)

    def stage2(c, carry):
        y = jnp.concatenate([yr_scr[pl.ds(c, FFT_R, stride=a), :], yi_scr[pl.ds(c, FFT_R, stride=a), :]], axis=0)
        o_ref[pl.ds(c, FFT_R, stride=a), :] = _dot(g2, y.astype(BF16))
        return carry

    lax.fori_loop(0, a, stage2, 0, unroll=8)


def _gqa_kernel(q_ref, k_ref, vt_ref, o_ref, st_a, st_b, *, tq, tk, seq):
    nk = seq // tk
    rep = (W_MIX // HEAD_DIM) // N_KV_GQA
    zeros = jnp.zeros((tq, HEAD_DIM), BF16)
    q2 = []
    for g in range(N_KV_GQA):
        parts = []
        for r in range(rep):
            h = g * rep + r
            qh = q_ref[:, h * HEAD_DIM:(h + 1) * HEAD_DIM]
            parts.append(jnp.concatenate([zeros] * g + [qh] + [zeros] * (N_KV_GQA - 1 - g), axis=1))
        q2.append(jnp.concatenate(parts, axis=0))
    rows = rep * tq
    vw = 2 * HEAD_DIM

    lanes = GQA_UNIT_LANES
    units = [(g, u) for g in range(N_KV_GQA) for u in range(rows // lanes)]
    q_unit = [q2[g][u * lanes:(u + 1) * lanes, :] for g, u in units]

    def scores(c, st_scr, i):
        off = pl.multiple_of(c * tk, tk)
        st_scr[i] = _dot_nt(k_ref[pl.ds(off, tk), :], q_unit[i])

    def softmax_pv(c, st_scr, i, state):
        off = pl.multiple_of(c * tk, tk)
        g = units[i][0]
        m, acc = state
        st = st_scr[i]
        vt = vt_ref[g * vw:(g + 1) * vw, pl.ds(off, tk)]
        m_new = jnp.maximum(m, jnp.max(st, axis=0, keepdims=True))
        alpha = jnp.exp2(m - m_new)
        pt = jnp.exp2(st - m_new).astype(BF16)
        return m_new, alpha * acc + _dot(vt, pt)

    def half(c_cur, c_next, st_cur, st_next, carry):
        out = []
        for i in range(len(units)):
            scores(c_next, st_next, i)
            out.append(softmax_pv(c_cur, st_cur, i, carry[i]))
        return tuple(out)

    def body(c2, carry):
        c = 2 * c2
        carry = half(c, c + 1, st_a, st_b, carry)
        return half(c + 1, (c + 2) % nk, st_b, st_a, carry)

    init = tuple((jnp.full((1, lanes), -jnp.inf, F32), jnp.zeros((vw, lanes), F32)) for _ in units)
    for i in range(len(units)):
        scores(0, st_a, i)
    final = lax.fori_loop(0, nk // 2, body, init)
    for g in range(N_KV_GQA):
        acc = jnp.concatenate([final[i][1] for i, (gi, _) in enumerate(units) if gi == g], axis=1)
        o = (acc[:HEAD_DIM] / acc[HEAD_DIM:HEAD_DIM + 1]).T
        for r in range(rep):
            h = g * rep + r
            o_ref[:, h * HEAD_DIM:(h + 1) * HEAD_DIM] = o[r * tq:(r + 1) * tq].astype(BF16)


def _conv_kernel(cur_ref, prev_ref, next_ref, dw_ref, cb_ref, g_ref, b_ref, wpw_ref, o_ref, hpad, shifted, *, ts, nblk):
    c = pl.program_id(0) % nblk

    def glu(u):
        return u[:, :W_MIX] * jax.nn.sigmoid(u[:, W_MIX:])

    hpad[0:HALO, :] = jnp.where(c > 0, glu(prev_ref[...]), 0.0)
    hpad[HALO:HALO + ts, :] = glu(cur_ref[...])
    hpad[HALO + ts:HALO + ts + HALO, :] = jnp.where(c < nblk - 1, glu(next_ref[...]), 0.0)
    sublanes = 8
    first = HALO - CONV_PAD
    span = ts + ((first + CONV_WIDTH - 1) // sublanes) * sublanes
    for b in range(1, sublanes):
        shifted[b - 1, 0:span, :] = hpad[b:b + span, :]
    acc = jnp.zeros((ts, W_MIX), F32)
    for j in range(CONV_WIDTH):
        start = first + j
        base = (start // sublanes) * sublanes
        b = start % sublanes
        tap = hpad[base:base + ts, :] if b == 0 else shifted[b - 1, base:base + ts, :]
        acc = acc + dw_ref[j:j + 1, :] * tap
    h = _layer_norm(acc + cb_ref[...], g_ref[...], b_ref[...])
    h = h * jax.nn.sigmoid(h)
    o_ref[...] = _dot(h.astype(BF16), wpw_ref[...]).astype(BF16)


def _bias_kernel(idx_ref, rb_ref, o_ref):
    idx = idx_ref[...]
    for h in range(W_MIX // HEAD_DIM):
        tile = jnp.full(idx.shape, NEG, F32)
        for b in range(REL_BUCKETS):
            tile = jnp.where(idx == b, rb_ref[b, h] * LOG2E, tile)
        o_ref[h] = tile


def _dilated_branch(q_in, k_in, v_in, bias_ref, qbuf, kbuf, vbuf, m_st, acc_st, s_a, s_b, *, dil, first, seq, tm, tq):
    n = DIL_N
    heads = W_MIX // HEAD_DIM
    length = seq // dil
    lp = length + 2 * n
    c = tm // dil
    win = tq + 2 * n
    nb = length // tq
    nblocks = dil * nb
    slab = 2 * HEAD_DIM

    lane = lax.broadcasted_iota(jnp.int32, (c, HEAD_DIM), 1)
    one_col = jnp.where(lane == 0, 1.0, 0.0).astype(BF16)
    for r in range(dil):
        base = r * lp
        for buf in (kbuf, vbuf):
            zeros = jnp.zeros((n, buf.shape[1]), BF16)
            buf[base:base + n, :] = zeros
            buf[base + n + length:base + lp, :] = zeros
        for t in range(seq // tm):
            src = slice(r * c, (r + 1) * c)
            qbuf[r * length + t * c:r * length + (t + 1) * c, :] = q_in[t, src, :]
            kbuf[base + n + t * c:base + n + (t + 1) * c, :] = k_in[t, src, :]
            vv = v_in[t, src, :]
            vbuf[base + n + t * c:base + n + (t + 1) * c, :] = jnp.concatenate(
                [piece for h in range(heads) for piece in (vv[:, h * HEAD_DIM:(h + 1) * HEAD_DIM], one_col)], axis=1)

    head_of_lane = lax.broadcasted_iota(jnp.int32, (tq, W_MIX), 1) // HEAD_DIM

    def block_pos(j):
        r = j // nb
        i = j - r * nb
        return r, i, pl.multiple_of(j * tq + r * 2 * n, 2 * n if tq % (2 * n) == 0 else tq)

    def scores(j, s_scr):
        _, _, krow = block_pos(j)
        qb = qbuf[pl.ds(pl.multiple_of(j * tq, tq), tq), :]
        kw = kbuf[pl.ds(krow, win), :]
        qs = jnp.concatenate([jnp.where(head_of_lane == h, qb, jnp.zeros_like(qb)) for h in range(heads)], axis=0)
        s_scr[...] = _dot_nt(qs, kw)

    def update(j, s_scr):
        r, i, krow = block_pos(j)
        vw = vbuf[pl.ds(krow, win), :]
        kpos = i * tq - n + lax.broadcasted_iota(jnp.int32, (tq, win), 1)
        valid = (kpos >= 0) & (kpos < length)
        rows = pl.ds(r + dil * i * tq, tq, stride=dil) if dil > 1 else pl.ds(pl.multiple_of(i * tq, tq), tq)
        for h in range(heads):
            s = jnp.where(valid, s_scr[h * tq:(h + 1) * tq, :] + bias_ref[h], NEG)
            mblk = jnp.max(s, axis=-1, keepdims=True)
            if first:
                m_new = jnp.broadcast_to(mblk, (tq, slab))
            else:
                m_old = m_st[h, rows, :]
                m_new = jnp.maximum(m_old, mblk)
            m_row = jnp.tile(m_new, (1, win // slab)) if win % slab == 0 else m_new[:, :1]
            p = jnp.exp2(s - m_row).astype(BF16)
            pv = _dot(p, vw[:, h * slab:(h + 1) * slab])
            if not first:
                pv = jnp.exp2(m_old - m_new) * acc_st[h, rows, :] + pv
            m_st[h, rows, :] = m_new
            acc_st[h, rows, :] = pv

    def body(j2, carry):
        j = 2 * j2
        scores(j + 1, s_b)
        update(j, s_a)
        scores((j + 2) % nblocks, s_a)
        update(j + 1, s_b)
        return carry

    scores(0, s_a)
    lax.fori_loop(0, nblocks // 2, body, 0)


def _dilated_kernel(q_in, k_in, v_in, bias_ref, o_ref, qbuf, kbuf, vbuf, m_st, acc_st, s_a, s_b, *, seq, tm, tq):
    br = pl.program_id(1)
    for idx, (_, dil) in enumerate(DIL_CONFIGS):
        @pl.when(br == idx)
        def _(idx=idx, dil=dil):
            _dilated_branch(q_in, k_in, v_in, bias_ref, qbuf, kbuf, vbuf, m_st, acc_st, s_a, s_b,
                            dil=dil, first=(idx == 0), seq=seq, tm=tm, tq=tq)

    @pl.when(br == len(DIL_CONFIGS) - 1)
    def _():
        heads = W_MIX // HEAD_DIM
        rows_per = 256

        def finish(t, carry):
            rows = pl.ds(pl.multiple_of(t * rows_per, rows_per), rows_per)
            outs = []
            for h in range(heads):
                a = acc_st[h, rows, :]
                outs.append(a[:, :HEAD_DIM] / a[:, HEAD_DIM:HEAD_DIM + 1])
            o_ref[rows, :] = jnp.concatenate(outs, axis=1).astype(BF16)
            return carry

        lax.fori_loop(0, seq // rows_per, finish, 0)


def _outproj_kernel(x_ref, ya_ref, yb_ref, yc_ref, yd_ref, w_ref, g_ref, b_ref, out_ref):
    y = _dot(ya_ref[...].astype(BF16), w_ref[0:W_MIX, :])
    y = y + _dot(yb_ref[...], w_ref[W_MIX:2 * W_MIX, :])
    y = y + _dot(yc_ref[...], w_ref[2 * W_MIX:3 * W_MIX, :])
    y = y + _dot(yd_ref[...], w_ref[3 * W_MIX:4 * W_MIX, :])
    out_ref[...] = _layer_norm(ALPHA * x_ref[...] + y, g_ref[...], b_ref[...])


def _ffn_kernel(x_ref, w1_ref, w2_ref, g_ref, b_ref, o_ref, acc_ref, *, nj):
    j = pl.program_id(1)
    sub = FFN_SUB_ROWS
    nsub = x_ref.shape[0] // sub

    def rows(r):
        return slice(r * sub, (r + 1) * sub)

    def up(r):
        return _dot(x_ref[rows(r), :].astype(BF16), w1_ref[...])

    def act(h):
        return jnp.square(jnp.maximum(h, 0.0)).astype(BF16)

    def chunk(first, last):
        h_next = act(up(0))
        for r in range(nsub):
            h_cur = h_next
            if r + 1 < nsub:
                raw = up(r + 1)
            y = _dot(h_cur, w2_ref[...])
            if not first:
                y = y + acc_ref[rows(r), :]
            if last:
                o_ref[rows(r), :] = _layer_norm(ALPHA * x_ref[rows(r), :] + y, g_ref[...], b_ref[...])
            else:
                acc_ref[rows(r), :] = y
            if r + 1 < nsub:
                h_next = act(raw)

    if nj == 1:
        chunk(True, True)
    else:
        pl.when(j == 0)(lambda: chunk(True, False))
        if nj > 2:
            pl.when((j > 0) & (j < nj - 1))(lambda: chunk(False, False))
        pl.when(j == nj - 1)(lambda: chunk(False, True))


def _tile_rows(n, want):
    t = min(n, want)
    assert n % t == 0, (n, t)
    return t


def _embed_ln(x, g, b):
    n, d = x.shape
    tm = _tile_rows(n, 512)
    return pl.pallas_call(
        _embed_ln_kernel, grid=(n // tm,), name="embed_ln",
        in_specs=[pl.BlockSpec((tm, d), lambda i: (i, 0)), pl.BlockSpec((1, d), lambda i: (0, 0)),
                  pl.BlockSpec((1, d), lambda i: (0, 0))],
        out_specs=pl.BlockSpec((tm, d), lambda i: (i, 0)),
        out_shape=jax.ShapeDtypeStruct((n, d), F32), compiler_params=_cparams(("parallel",)),
    )(x, g.reshape(1, d), b.reshape(1, d))


def _fold_fourier_weights(w_in, w_fnet, seq):
    c_bd, s_bd = _channel_dft_blockdiag(seq)
    depth = w_in.shape[0]
    sq = pl.BlockSpec((W_MIX, W_MIX), lambda l: (0, 0))
    return pl.pallas_call(
        _fold_kernel, grid=(depth,), name="fold_fourier_weights",
        in_specs=[pl.BlockSpec((None, D_MODEL, W_MIX), lambda l: (l, 0, 0)),
                  pl.BlockSpec((None, W_MIX, W_MIX), lambda l: (l, 0, 0)), sq, sq],
        out_specs=pl.BlockSpec((None, D_MODEL, 2 * W_MIX), lambda l: (l, 0, 0)),
        out_shape=jax.ShapeDtypeStruct((depth, D_MODEL, 2 * W_MIX), BF16),
        compiler_params=_cparams(("parallel",)),
    )(w_in, w_fnet, jnp.asarray(c_bd), jnp.asarray(s_bd))


def _inproj(x, wfold, w_in_bf, wvt_bf, cos, sin, gq, gk, ones, layer, batch, seq):
    n = x.shape[0]
    tm = _tile_rows(seq, INPROJ_ROWS)
    kvw = N_KV_GQA * HEAD_DIM
    pos_blocks = seq // tm
    nbr = len(DIL_CONFIGS)
    assert DIL_CONFIGS[0][1] == 1 and all(tm % (dil * 16) == 0 for _, dil in DIL_CONFIGS)

    def rows(c):
        return pl.BlockSpec((tm, c), lambda i: (i, 0))

    def whole(r, c):
        return pl.BlockSpec((r, c), lambda i: (0, 0))

    def flat(c, dt):
        return rows(c), jax.ShapeDtypeStruct((n, c), dt)

    dil_out = (pl.BlockSpec((None, None, nbr, tm, W_MIX), lambda i: (i // pos_blocks, i % pos_blocks, 0, 0, 0)),
               jax.ShapeDtypeStruct((batch, pos_blocks, nbr, tm, W_MIX), BF16))
    vt_out = (pl.BlockSpec((2 * kvw, tm), lambda i: (0, i)),
              jax.ShapeDtypeStruct((2 * kvw, n), BF16))
    outs = [flat(W_MIX, F32), flat(W_MIX, F32), flat(W_MIX, BF16), flat(kvw, BF16), vt_out, flat(2 * W_MIX, F32),
            dil_out, dil_out, dil_out]
    return pl.pallas_call(
        _inproj_kernel, grid=(n // tm,), name="inproj",
        in_specs=[rows(D_MODEL),
                  pl.BlockSpec((None, D_MODEL, 2 * W_MIX), lambda i: (layer, 0, 0)),
                  pl.BlockSpec((None, D_MODEL, D_IN), lambda i: (layer, 0, 0)),
                  pl.BlockSpec((None, kvw, D_MODEL), lambda i: (layer, 0, 0)),
                  pl.BlockSpec((tm, W_MIX), lambda i: (i % pos_blocks, 0)),
                  pl.BlockSpec((tm, W_MIX), lambda i: (i % pos_blocks, 0)),
                  pl.BlockSpec((None, 1, W_MIX), lambda i: (layer, 0, 0)),
                  pl.BlockSpec((None, 1, kvw), lambda i: (layer, 0, 0)),
                  whole(W_MIX, W_MIX)],
        out_specs=[o[0] for o in outs], out_shape=[o[1] for o in outs],
        scratch_shapes=[pltpu.VMEM((W_MIX // 128, tm, 128), F32)] * 3,
        compiler_params=_cparams(("parallel",)),
    )(x, wfold, w_in_bf, wvt_bf, cos, sin, gq, gk, ones)


def _fourier(zr, zi, g1, g2, twr, twi, batch, seq):
    a = seq // FFT_R
    blk = pl.BlockSpec((seq, 128), lambda b, j: (b, j))

    def whole(shape):
        return pl.BlockSpec(shape, lambda b, j: (0,) * len(shape))

    return pl.pallas_call(
        functools.partial(_fourier_kernel, a=a), grid=(batch, W_MIX // 128), name="fourier_mix",
        in_specs=[blk, blk, whole((2 * a, 2 * a)), whole((FFT_R, 2 * FFT_R)), whole((FFT_R, a, 128)),
                  whole((FFT_R, a, 128))],
        out_specs=blk, out_shape=jax.ShapeDtypeStruct((batch * seq, W_MIX), F32),
        scratch_shapes=[pltpu.VMEM((seq, 128), F32)] * 2,
        compiler_params=_cparams(("parallel", "parallel")),
    )(zr, zi, g1, g2, twr, twi)


def _gqa(qn, kn, v, batch, seq):
    tq = _tile_rows(seq, 512)
    tk = _tile_rows(seq, 256)
    nq = seq // tq
    kvw = N_KV_GQA * HEAD_DIM
    rep = (W_MIX // HEAD_DIM) // N_KV_GQA
    assert (seq // tk) % 2 == 0
    return pl.pallas_call(
        functools.partial(_gqa_kernel, tq=tq, tk=tk, seq=seq), grid=(batch, nq), name="gqa_attention",
        in_specs=[pl.BlockSpec((tq, W_MIX), lambda b, i: (b * nq + i, 0)),
                  pl.BlockSpec((seq, kvw), lambda b, i: (b, 0)),
                  pl.BlockSpec((2 * kvw, seq), lambda b, i: (0, b))],
        out_specs=pl.BlockSpec((tq, W_MIX), lambda b, i: (b * nq + i, 0)),
        out_shape=jax.ShapeDtypeStruct((batch * seq, W_MIX), BF16),
        scratch_shapes=[pltpu.VMEM((N_KV_GQA * rep * tq // GQA_UNIT_LANES, tk, GQA_UNIT_LANES), F32)] * 2,
        compiler_params=_cparams(("parallel", "parallel")),
    )(qn, kn, v)


def _conv(uc, dw, cb, g, b, wpw_bf, layer, seq):
    n = uc.shape[0]
    ts = _tile_rows(seq, 512)
    nblk = seq // ts
    hb = ts // HALO
    last = n // HALO - 1

    def vec():
        return pl.BlockSpec((None, 1, W_MIX), lambda i: (layer, 0, 0))

    return pl.pallas_call(
        functools.partial(_conv_kernel, ts=ts, nblk=nblk), grid=(n // ts,), name="conformer_conv",
        in_specs=[pl.BlockSpec((ts, 2 * W_MIX), lambda i: (i, 0)),
                  pl.BlockSpec((HALO, 2 * W_MIX), lambda i: (jnp.maximum(i * hb - 1, 0), 0)),
                  pl.BlockSpec((HALO, 2 * W_MIX), lambda i: (jnp.minimum((i + 1) * hb, last), 0)),
                  pl.BlockSpec((None, CONV_WIDTH, W_MIX), lambda i: (layer, 0, 0)),
                  vec(), vec(), vec(),
                  pl.BlockSpec((None, W_MIX, W_MIX), lambda i: (layer, 0, 0))],
        out_specs=pl.BlockSpec((ts, W_MIX), lambda i: (i, 0)),
        out_shape=jax.ShapeDtypeStruct((n, W_MIX), BF16),
        scratch_shapes=[pltpu.VMEM((ts + 2 * HALO, W_MIX), F32), pltpu.VMEM((7, ts + 2 * HALO, W_MIX), F32)],
        compiler_params=_cparams(("parallel",)),
    )(uc, uc, uc, dw, cb, g, b, wpw_bf)


def _bias_tiles(rel_bias, tq):
    idx = jnp.asarray(_bucket_tiles(tq))
    nbr, _, win = idx.shape
    heads = W_MIX // HEAD_DIM
    return pl.pallas_call(
        _bias_kernel, grid=(nbr,), name="dilated_bias_tiles",
        in_specs=[pl.BlockSpec((None, tq, win), lambda r: (r, 0, 0)),
                  pl.BlockSpec(memory_space=pltpu.SMEM)],
        out_specs=pl.BlockSpec((None, heads, tq, win), lambda r: (r, 0, 0, 0)),
        out_shape=jax.ShapeDtypeStruct((nbr, heads, tq, win), F32),
        compiler_params=_cparams(("parallel",)),
    )(idx, rel_bias)


def _dilated(qd, kd, vd, bias, tq, batch, seq):
    heads = W_MIX // HEAD_DIM
    win = tq + 2 * DIL_N
    tiles, nbr, tm = qd.shape[1], qd.shape[2], qd.shape[3]
    max_dil = max(dil for _, dil in DIL_CONFIGS)
    pad_rows = seq + max_dil * 2 * DIL_N
    assert (seq // tq) % 2 == 0
    blk = pl.BlockSpec((None, tiles, None, tm, W_MIX), lambda b, r: (b, 0, r, 0, 0))
    return pl.pallas_call(
        functools.partial(_dilated_kernel, seq=seq, tm=tm, tq=tq), grid=(batch, nbr), name="dilated_attention",
        in_specs=[blk, blk, blk, pl.BlockSpec((None, heads, tq, win), lambda b, r: (r, 0, 0, 0))],
        out_specs=pl.BlockSpec((seq, W_MIX), lambda b, r: (b, 0)),
        out_shape=jax.ShapeDtypeStruct((batch * seq, W_MIX), BF16),
        scratch_shapes=[pltpu.VMEM((seq, W_MIX), BF16), pltpu.VMEM((pad_rows, W_MIX), BF16),
                        pltpu.VMEM((pad_rows, 2 * W_MIX), BF16),
                        pltpu.VMEM((heads, seq, 2 * HEAD_DIM), F32), pltpu.VMEM((heads, seq, 2 * HEAD_DIM), F32),
                        pltpu.VMEM((heads * tq, win), F32), pltpu.VMEM((heads * tq, win), F32)],
        compiler_params=_cparams(("parallel", "arbitrary")),
    )(qd, kd, vd, bias)


def _outproj(x, ya, yb, yc, yd, w_out_bf, g, b, layer):
    n = x.shape[0]
    tm = _tile_rows(n, 512)

    def rows(c):
        return pl.BlockSpec((tm, c), lambda i: (i, 0))

    def vec():
        return pl.BlockSpec((None, 1, D_MODEL), lambda i: (layer, 0, 0))

    return pl.pallas_call(
        _outproj_kernel, grid=(n // tm,), name="outproj_ln",
        in_specs=[rows(D_MODEL)] + [rows(W_MIX)] * 4
                 + [pl.BlockSpec((None, D_MODEL, D_MODEL), lambda i: (layer, 0, 0)), vec(), vec()],
        out_specs=rows(D_MODEL), out_shape=jax.ShapeDtypeStruct((n, D_MODEL), F32),
        compiler_params=_cparams(("parallel",)),
    )(x, ya, yb, yc, yd, w_out_bf, g, b)


def _ffn(x, w1_bf, w2_bf, g, b, layer):
    n = x.shape[0]
    tm = _tile_rows(n, 1024)
    tf = FFN_CHUNK

    def vec():
        return pl.BlockSpec((None, 1, D_MODEL), lambda i, j: (layer, 0, 0))

    return pl.pallas_call(
        functools.partial(_ffn_kernel, nj=D_FF // tf), grid=(n // tm, D_FF // tf), name="ffn_ln",
        in_specs=[pl.BlockSpec((tm, D_MODEL), lambda i, j: (i, 0)),
                  pl.BlockSpec((None, D_MODEL, tf), lambda i, j: (layer, 0, j)),
                  pl.BlockSpec((None, tf, D_MODEL), lambda i, j: (layer, j, 0)), vec(), vec()],
        out_specs=pl.BlockSpec((tm, D_MODEL), lambda i, j: (i, 0)),
        out_shape=jax.ShapeDtypeStruct((n, D_MODEL), F32),
        scratch_shapes=[pltpu.VMEM((tm, D_MODEL), F32)],
        compiler_params=_cparams(("parallel", "arbitrary")),
    )(x, w1_bf, w2_bf, g, b)


def kernel(x, emb_ln_g, emb_ln_b, w_in, w_fnet, q_norm_g, k_norm_g, conv_dw, conv_b, conv_ln_g, conv_ln_b,
           w_conv_out, w_out, ln1_g, ln1_b, w_ff1, w_ff2, ln2_g, ln2_b, rel_bias):
    batch, seq, d = x.shape
    depth = w_in.shape[0]
    assert d == D_MODEL and seq % (FFT_R * 16) == 0 and seq % (DIL_CONFIGS[-1][1] * DIL_N) == 0
    n = batch * seq
    a = seq // FFT_R
    heads = W_MIX // HEAD_DIM
    kvw = N_KV_GQA * HEAD_DIM

    cos_np, sin_np = _rope_tables(seq)
    cos, sin = jnp.asarray(cos_np), jnp.asarray(sin_np)
    ones = jnp.asarray(np.kron(np.eye(heads), np.ones((HEAD_DIM, HEAD_DIM))), BF16)
    g1_np, g2_np = _fft_stage_mats(a)
    g1, g2 = jnp.asarray(g1_np, BF16), jnp.asarray(g2_np, BF16)
    twr_np, twi_np = _twiddle(a, seq)
    twr = jnp.broadcast_to(jnp.asarray(twr_np.T)[:, :, None], (FFT_R, a, 128))
    twi = jnp.broadcast_to(jnp.asarray(twi_np.T)[:, :, None], (FFT_R, a, 128))

    w_in_bf = w_in.astype(BF16)
    wvt_bf = jnp.swapaxes(w_in_bf[:, :, COL_KVB + kvw:COL_GLU], 1, 2)
    w_out_bf = w_out.astype(BF16)
    w_ff1_bf = w_ff1.astype(BF16)
    w_ff2_bf = w_ff2.astype(BF16)
    wpw_bf = w_conv_out.astype(BF16)
    gq = jnp.tile(q_norm_g, (1, heads)).reshape(depth, 1, W_MIX)
    gk = jnp.tile(k_norm_g, (1, N_KV_GQA)).reshape(depth, 1, kvw)
    r3 = lambda t: t.reshape(depth, 1, t.shape[-1])

    wfold = _fold_fourier_weights(w_in, w_fnet, seq)
    tq_d = min(128, seq // DIL_CONFIGS[-1][1])
    bias = _bias_tiles(rel_bias, tq_d)

    h = _embed_ln(x.reshape(n, d), emb_ln_g, emb_ln_b)
    for l in range(depth):
        zr, zi, qn, kn, v, uc, qd, kd, vd = _inproj(h, wfold, w_in_bf, wvt_bf, cos, sin, gq, gk, ones, l, batch, seq)
        ya = _fourier(zr, zi, g1, g2, twr, twi, batch, seq)
        yb = _gqa(qn, kn, v, batch, seq)
        yc = _conv(uc, conv_dw, r3(conv_b), r3(conv_ln_g), r3(conv_ln_b), wpw_bf, l, seq)
        yd = _dilated(qd, kd, vd, bias, tq_d, batch, seq)
        h = _outproj(h, ya, yb, yc, yd, w_out_bf, r3(ln1_g), r3(ln1_b), l)
        h = _ffn(h, w_ff1_bf, w_ff2_bf, r3(ln2_g), r3(ln2_b), l)
    return h.reshape(batch, seq, d)
```

```python
import functools
import math

import numpy as np
import jax
import jax.numpy as jnp
from jax import lax
from jax.experimental import pallas as pl
from jax.experimental.pallas import tpu as pltpu

F32 = jnp.float32
BF16 = jnp.bfloat16

D_MODEL = 1024
DEPTH = 4
HEAD_DIM = 64
W_MIX = 256
N_KV_GQA = 2
CONV_WIDTH = 31
CONV_PAD = CONV_WIDTH // 2
DIL_CONFIGS = ((128, 1), (512, 4), (2048, 16))
DIL_N = 64
MAX_DIL = 16
DIL_TQ = 128
D_FF = 4 * D_MODEL
GRID_W = 64
ROPE_THETA = 10000.0
REL_BUCKETS = 32
REL_MAX_DIST = 1024
ALPHA = (2 * DEPTH) ** 0.25
LN_EPS = 1e-5
RMS_EPS = 1e-6
NEG = -1e30
ATTN_SCALE = HEAD_DIM ** -0.5
LOG2E = math.log2(math.e)

COL_QB, COL_KVB, COL_GLU, COL_QD, COL_KD, COL_VD, D_IN = 256, 512, 768, 1280, 1536, 1792, 2048

FFT_R = 64
INPROJ_ROWS = 1024
GQA_UNIT_LANES = 256
FFN_CHUNK = 1024
FFN_SUB_ROWS = 256
HALO = 16
VMEM_LIMIT = 56 * 1024 * 1024


def _cparams(sem):
    return pltpu.CompilerParams(dimension_semantics=sem, vmem_limit_bytes=VMEM_LIMIT)


def _dft_cos_sin(n):
    j = np.arange(n)
    ang = 2.0 * np.pi * ((j[:, None] * j[None, :]) % n) / n
    return np.cos(ang), np.sin(ang)


def _channel_dft_blockdiag(seq):
    c, s = _dft_cos_sin(HEAD_DIM)
    scale = 1.0 / math.sqrt(seq * HEAD_DIM)
    eye = np.eye(W_MIX // HEAD_DIM)
    return (np.kron(eye, c) * scale).astype(np.float32), (np.kron(eye, -s) * scale).astype(np.float32)


def _fft_stage_mats(a):
    ca, sa = _dft_cos_sin(a)
    g1 = np.block([[ca, sa], [-sa, ca]])
    c2, s2 = _dft_cos_sin(FFT_R)
    g2 = np.concatenate([c2, s2], axis=1)
    return g1.astype(np.float32), g2.astype(np.float32)


def _twiddle(a, seq):
    c = np.arange(a)[:, None]
    b = np.arange(FFT_R)[None, :]
    ang = 2.0 * np.pi * ((b * c) % seq) / seq
    return np.cos(ang).astype(np.float32), (-np.sin(ang)).astype(np.float32)


def _rope_tables(seq):
    rows = seq // GRID_W
    row = np.repeat(np.arange(rows), GRID_W).astype(np.float32)
    col = np.tile(np.arange(GRID_W), rows).astype(np.float32)
    nf = HEAD_DIM // 4
    inv = (ROPE_THETA ** (-np.arange(nf, dtype=np.float32) / nf)).astype(np.float32)
    ar = row[:, None] * inv
    ac = col[:, None] * inv
    cos = np.concatenate([np.cos(ar), np.cos(ar), np.cos(ac), np.cos(ac)], -1)
    sin = np.concatenate([-np.sin(ar), np.sin(ar), -np.sin(ac), np.sin(ac)], -1)
    reps = W_MIX // HEAD_DIM
    return np.tile(cos, (1, reps)).astype(np.float32), np.tile(sin, (1, reps)).astype(np.float32)


def _t5_bucket_np(rel):
    nb = REL_BUCKETS // 2
    max_exact = nb // 2
    ret = np.where(rel > 0, nb, 0)
    n = np.abs(rel)
    nf = np.maximum(n, 1).astype(np.float32)
    large = max_exact + (np.log(nf / np.float32(max_exact)) / np.float32(math.log(REL_MAX_DIST / max_exact))
                         * np.float32(nb - max_exact)).astype(np.int32)
    large = np.minimum(large, nb - 1)
    return ret + np.where(n < max_exact, n, large)


def _bucket_tiles(tq):
    qi = np.arange(tq)[:, None]
    ki = np.arange(tq + 2 * DIL_N)[None, :]
    rel = ki - DIL_N - qi
    tiles = []
    for _, dil in DIL_CONFIGS:
        tiles.append(np.where(np.abs(rel) <= DIL_N, _t5_bucket_np(rel * dil), -1))
    return np.stack(tiles).astype(np.int32)


def _layer_norm(r, g, b):
    mu = jnp.mean(r, axis=-1, keepdims=True)
    d = r - mu
    var = jnp.mean(d * d, axis=-1, keepdims=True)
    return d * lax.rsqrt(var + LN_EPS) * g + b


def _dot(a, b):
    return jnp.dot(a, b, preferred_element_type=F32)


def _dot_nt(a, b):
    return lax.dot_general(a, b, (((1,), (1,)), ((), ())), preferred_element_type=F32)


def _norm_rope(x, g, cos, sin, ones):
    x2 = x * x
    hi = x2.astype(BF16)
    lo = (x2 - hi.astype(F32)).astype(BF16)
    ss = _dot(hi, ones) + _dot(lo, ones)
    xn = x * lax.rsqrt(ss * (1.0 / HEAD_DIM) + RMS_EPS) * g
    w = x.shape[1]
    lane = lax.broadcasted_iota(jnp.int32, x.shape, 1)
    quarter = HEAD_DIM // 4
    partner = jnp.where((lane % (2 * quarter)) < quarter,
                        pltpu.roll(xn, w - quarter, 1), pltpu.roll(xn, quarter, 1))
    return xn * cos + partner * sin


def _embed_ln_kernel(x_ref, g_ref, b_ref, o_ref):
    o_ref[...] = _layer_norm(x_ref[...], g_ref[...], b_ref[...])


def _fold_kernel(win_ref, wf_ref, c_ref, s_ref, o_ref):
    hp = lax.Precision.HIGHEST
    wf = wf_ref[...]
    mc = jnp.dot(c_ref[...], wf, precision=hp, preferred_element_type=F32)
    ms = jnp.dot(s_ref[...], wf, precision=hp, preferred_element_type=F32)
    win = win_ref[...]
    o_ref[:, :W_MIX] = jnp.dot(win, mc, precision=hp, preferred_element_type=F32).astype(BF16)
    o_ref[:, W_MIX:] = jnp.dot(win, ms, precision=hp, preferred_element_type=F32).astype(BF16)


def _inproj_kernel(x_ref, wfold_ref, w_ref, wvt_ref, cos_ref, sin_ref, gq_ref, gk_ref, ones_ref,
                   zr_ref, zi_ref, qn_ref, kn_ref, vt_ref, uc_ref, qd_ref, kd_ref, vd_ref, stage_q, stage_k, stage_v):
    xb = x_ref[...].astype(BF16)
    kvw = N_KV_GQA * HEAD_DIM
    tm = xb.shape[0]
    q = _dot(xb, w_ref[:, COL_QB:COL_KVB])
    k = _dot(xb, w_ref[:, COL_KVB:COL_KVB + kvw])
    z = _dot(xb, wfold_ref[...])
    vt = _dot_nt(wvt_ref[...], xb)
    dil_u = [_dot(xb, w_ref[:, col:col + W_MIX]) for col in (COL_QD, COL_KD, COL_VD)]
    uc_ref[...] = _dot(xb, w_ref[:, COL_GLU:COL_QD])

    zr_ref[...] = z[:, :W_MIX]
    zi_ref[...] = z[:, W_MIX:]
    row = lax.broadcasted_iota(jnp.int32, (HEAD_DIM, tm), 0)
    one_row = jnp.where(row == 0, 1.0, 0.0).astype(BF16)
    vt_ref[...] = jnp.concatenate(
        [piece for g in range(N_KV_GQA)
         for piece in (vt[g * HEAD_DIM:(g + 1) * HEAD_DIM].astype(BF16), one_row)], axis=0)

    halves = W_MIX // 128
    for u, is_q, out_ref, stage in zip(dil_u, (True, False, False), (qd_ref, kd_ref, vd_ref),
                                       (stage_q, stage_k, stage_v)):
        if is_q:
            u = u * (ATTN_SCALE * LOG2E)
        for half in range(halves):
            stage[half] = u[:, half * 128:(half + 1) * 128]
        for br, (_, dil) in enumerate(DIL_CONFIGS):
            if dil == 1:
                out_ref[br] = u.astype(BF16)
                continue
            c = tm // dil
            for r in range(dil):
                out_ref[br, r * c:(r + 1) * c, :] = jnp.concatenate(
                    [stage[half, pl.ds(r, c, stride=dil), :] for half in range(halves)], axis=1).astype(BF16)

    cos = cos_ref[...]
    sin = sin_ref[...]
    ones = ones_ref[...]
    qn_ref[...] = (_norm_rope(q, gq_ref[...], cos, sin, ones) * (ATTN_SCALE * LOG2E)).astype(BF16)
    kn_ref[...] = _norm_rope(k, gk_ref[...], cos[:, :kvw], sin[:, :kvw], ones[:kvw, :kvw]).astype(BF16)


def _fourier_kernel(zr_ref, zi_ref, g1_ref, g2_ref, twr_ref, twi_ref, o_ref, yr_scr, yi_scr, *, a):
    g1 = g1_ref[...]
    g2 = g2_ref[...]

    def stage1(b, carry):
        z = jnp.concatenate([zr_ref[pl.ds(b, a, stride=FFT_R), :], zi_ref[pl.ds(b, a, stride=FFT_R), :]], axis=0)
        y = _dot(g1, z.astype(BF16))
        yr, yi = y[:a], y[a:]
        twr = twr_ref[b]
        twi = twi_ref[b]
        row0 = pl.multiple_of(b * a, a)
        yr_scr[pl.ds(row0, a), :] = yr * twr - yi * twi
        yi_scr[pl.ds(row0, a), :] = yr * twi + yi * twr
        return carry

    lax.fori_loop(0, FFT_R, stage1, 0, unroll=8)

    def stage2(c, carry):
        y = jnp.concatenate([yr_scr[pl.ds(c, FFT_R, stride=a), :], yi_scr[pl.ds(c, FFT_R, stride=a), :]], axis=0)
        o_ref[pl.ds(c, FFT_R, stride=a), :] = _dot(g2, y.astype(BF16))
        return carry

    lax.fori_loop(0, a, stage2, 0, unroll=8)


def _gqa_kernel(q_ref, k_ref, vt_ref, o_ref, st_a, st_b, *, tq, tk, seq):
    nk = seq // tk
    rep = (W_MIX // HEAD_DIM) // N_KV_GQA
    zeros = jnp.zeros((tq, HEAD_DIM), BF16)
    q2 = []
    for g in range(N_KV_GQA):
        parts = []
        for r in range(rep):
            h = g * rep + r
            qh = q_ref[:, h * HEAD_DIM:(h + 1) * HEAD_DIM]
            parts.append(jnp.concatenate([zeros] * g + [qh] + [zeros] * (N_KV_GQA - 1 - g), axis=1))
        q2.append(jnp.concatenate(parts, axis=0))
    rows = rep * tq
    vw = 2 * HEAD_DIM

    lanes = GQA_UNIT_LANES
    units = [(g, u) for g in range(N_KV_GQA) for u in range(rows // lanes)]
    q_unit = [q2[g][u * lanes:(u + 1) * lanes, :] for g, u in units]

    def scores(c, st_scr, i):
        off = pl.multiple_of(c * tk, tk)
        st_scr[i] = _dot_nt(k_ref[pl.ds(off, tk), :], q_unit[i])

    def softmax_pv(c, st_scr, i, state):
        off = pl.multiple_of(c * tk, tk)
        g = units[i][0]
        m, acc = state
        st = st_scr[i]
        vt = vt_ref[g * vw:(g + 1) * vw, pl.ds(off, tk)]
        m_new = jnp.maximum(m, jnp.max(st, axis=0, keepdims=True))
        alpha = jnp.exp2(m - m_new)
        pt = jnp.exp2(st - m_new).astype(BF16)
        return m_new, alpha * acc + _dot(vt, pt)

    def half(c_cur, c_next, st_cur, st_next, carry):
        out = []
        for i in range(len(units)):
            scores(c_next, st_next, i)
            out.append(softmax_pv(c_cur, st_cur, i, carry[i]))
        return tuple(out)

    def body(c2, carry):
        c = 2 * c2
        carry = half(c, c + 1, st_a, st_b, carry)
        return half(c + 1, (c + 2) % nk, st_b, st_a, carry)

    init = tuple((jnp.full((1, lanes), -jnp.inf, F32), jnp.zeros((vw, lanes), F32)) for _ in units)
    for i in range(len(units)):
        scores(0, st_a, i)
    final = lax.fori_loop(0, nk // 2, body, init)
    for g in range(N_KV_GQA):
        acc = jnp.concatenate([final[i][1] for i, (gi, _) in enumerate(units) if gi == g], axis=1)
        o = (acc[:HEAD_DIM] / acc[HEAD_DIM:HEAD_DIM + 1]).T
        for r in range(rep):
            h = g * rep + r
            o_ref[:, h * HEAD_DIM:(h + 1) * HEAD_DIM] = o[r * tq:(r + 1) * tq].astype(BF16)


def _conv_kernel(cur_ref, prev_ref, next_ref, dw_ref, cb_ref, g_ref, b_ref, wpw_ref, o_ref, hpad, shifted, *, ts, nblk):
    c = pl.program_id(0) % nblk

    def glu(u):
        return u[:, :W_MIX] * jax.nn.sigmoid(u[:, W_MIX:])

    hpad[0:HALO, :] = jnp.where(c > 0, glu(prev_ref[...]), 0.0)
    hpad[HALO:HALO + ts, :] = glu(cur_ref[...])
    hpad[HALO + ts:HALO + ts + HALO, :] = jnp.where(c < nblk - 1, glu(next_ref[...]), 0.0)
    sublanes = 8
    first = HALO - CONV_PAD
    span = ts + ((first + CONV_WIDTH - 1) // sublanes) * sublanes
    for b in range(1, sublanes):
        shifted[b - 1, 0:span, :] = hpad[b:b + span, :]
    acc = jnp.zeros((ts, W_MIX), F32)
    for j in range(CONV_WIDTH):
        start = first + j
        base = (start // sublanes) * sublanes
        b = start % sublanes
        tap = hpad[base:base + ts, :] if b == 0 else shifted[b - 1, base:base + ts, :]
        acc = acc + dw_ref[j:j + 1, :] * tap
    h = _layer_norm(acc + cb_ref[...], g_ref[...], b_ref[...])
    h = h * jax.nn.sigmoid(h)
    o_ref[...] = _dot(h.astype(BF16), wpw_ref[...]).astype(BF16)


def _bias_kernel(idx_ref, rb_ref, o_ref):
    idx = idx_ref[...]
    for h in range(W_MIX // HEAD_DIM):
        tile = jnp.full(idx.shape, NEG, F32)
        for b in range(REL_BUCKETS):
            tile = jnp.where(idx == b, rb_ref[b, h] * LOG2E, tile)
        o_ref[h] = tile


def _dilated_branch(q_in, k_in, v_in, bias_ref, qbuf, kbuf, vbuf, m_st, acc_st, s_a, s_b, *, dil, first, seq, tm, tq):
    n = DIL_N
    heads = W_MIX // HEAD_DIM
    length = seq // dil
    lp = length + 2 * n
    c = tm // dil
    win = tq + 2 * n
    nb = length // tq
    nblocks = dil * nb
    slab = 2 * HEAD_DIM

    lane = lax.broadcasted_iota(jnp.int32, (c, HEAD_DIM), 1)
    one_col = jnp.where(lane == 0, 1.0, 0.0).astype(BF16)
    for r in range(dil):
        base = r * lp
        for buf in (kbuf, vbuf):
            zeros = jnp.zeros((n, buf.shape[1]), BF16)
            buf[base:base + n, :] = zeros
            buf[base + n + length:base + lp, :] = zeros
        for t in range(seq // tm):
            src = slice(r * c, (r + 1) * c)
            qbuf[r * length + t * c:r * length + (t + 1) * c, :] = q_in[t, src, :]
            kbuf[base + n + t * c:base + n + (t + 1) * c, :] = k_in[t, src, :]
            vv = v_in[t, src, :]
            vbuf[base + n + t * c:base + n + (t + 1) * c, :] = jnp.concatenate(
                [piece for h in range(heads) for piece in (vv[:, h * HEAD_DIM:(h + 1) * HEAD_DIM], one_col)], axis=1)

    head_of_lane = lax.broadcasted_iota(jnp.int32, (tq, W_MIX), 1) // HEAD_DIM

    def block_pos(j):
        r = j // nb
        i = j - r * nb
        return r, i, pl.multiple_of(j * tq + r * 2 * n, 2 * n if tq % (2 * n) == 0 else tq)

    def scores(j, s_scr):
        _, _, krow = block_pos(j)
        qb = qbuf[pl.ds(pl.multiple_of(j * tq, tq), tq), :]
        kw = kbuf[pl.ds(krow, win), :]
        qs = jnp.concatenate([jnp.where(head_of_lane == h, qb, jnp.zeros_like(qb)) for h in range(heads)], axis=0)
        s_scr[...] = _dot_nt(qs, kw)

    def update(j, s_scr):
        r, i, krow = block_pos(j)
        vw = vbuf[pl.ds(krow, win), :]
        kpos = i * tq - n + lax.broadcasted_iota(jnp.int32, (tq, win), 1)
        valid = (kpos >= 0) & (kpos < length)
        rows = pl.ds(r + dil * i * tq, tq, stride=dil) if dil > 1 else pl.ds(pl.multiple_of(i * tq, tq), tq)
        for h in range(heads):
            s = jnp.where(valid, s_scr[h * tq:(h + 1) * tq, :] + bias_ref[h], NEG)
            mblk = jnp.max(s, axis=-1, keepdims=True)
            if first:
                m_new = jnp.broadcast_to(mblk, (tq, slab))
            else:
                m_old = m_st[h, rows, :]
                m_new = jnp.maximum(m_old, mblk)
            m_row = jnp.tile(m_new, (1, win // slab)) if win % slab == 0 else m_new[:, :1]
            p = jnp.exp2(s - m_row).astype(BF16)
            pv = _dot(p, vw[:, h * slab:(h + 1) * slab])
            if not first:
                pv = jnp.exp2(m_old - m_new) * acc_st[h, rows, :] + pv
            m_st[h, rows, :] = m_new
            acc_st[h, rows, :] = pv

    def body(j2, carry):
        j = 2 * j2
        scores(j + 1, s_b)
        update(j, s_a)
        scores((j + 2) % nblocks, s_a)
        update(j + 1, s_b)
        return carry

    scores(0, s_a)
    lax.fori_loop(0, nblocks // 2, body, 0)


def _dilated_kernel(q_in, k_in, v_in, bias_ref, o_ref, qbuf, kbuf, vbuf, m_st, acc_st, s_a, s_b, *, seq, tm, tq):
    br = pl.program_id(1)
    for idx, (_, dil) in enumerate(DIL_CONFIGS):
        @pl.when(br == idx)
        def _(idx=idx, dil=dil):
            _dilated_branch(q_in, k_in, v_in, bias_ref, qbuf, kbuf, vbuf, m_st, acc_st, s_a, s_b,
                            dil=dil, first=(idx == 0), seq=seq, tm=tm, tq=tq)

    @pl.when(br == len(DIL_CONFIGS) - 1)
    def _():
        heads = W_MIX // HEAD_DIM
        rows_per = 256

        def finish(t, carry):
            rows = pl.ds(pl.multiple_of(t * rows_per, rows_per), rows_per)
            outs = []
            for h in range(heads):
                a = acc_st[h, rows, :]
                outs.append(a[:, :HEAD_DIM] / a[:, HEAD_DIM:HEAD_DIM + 1])
            o_ref[rows, :] = jnp.concatenate(outs, axis=1).astype(BF16)
            return carry

        lax.fori_loop(0, seq // rows_per, finish, 0)


def _mix_ffn_kernel(x_ref, ya_ref, yb_ref, yc_ref, yd_ref, wo_ref, g1_ref, b1_ref, w1_ref, w2_ref, g2_ref, b2_ref,
                    o_ref, x1_ref, acc_ref, *, nj):
    j = pl.program_id(1)
    sub = FFN_SUB_ROWS
    nsub = x_ref.shape[0] // sub

    def rows(r):
        return slice(r * sub, (r + 1) * sub)

    def mix(r):
        y_in = jnp.concatenate([ya_ref[rows(r), :].astype(BF16), yb_ref[rows(r), :], yc_ref[rows(r), :],
                                yd_ref[rows(r), :]], axis=1)
        x1 = _layer_norm(ALPHA * x_ref[rows(r), :] + _dot(y_in, wo_ref[...]), g1_ref[...], b1_ref[...])
        x1_ref[rows(r), :] = x1
        return x1.astype(BF16)

    def act(h):
        return jnp.square(jnp.maximum(h, 0.0)).astype(BF16)

    def chunk(first, last):
        def lhs(r):
            return mix(r) if first else x1_ref[rows(r), :].astype(BF16)

        h_next = act(_dot(lhs(0), w1_ref[...]))
        for r in range(nsub):
            h_cur = h_next
            if r + 1 < nsub:
                lhs_next = lhs(r + 1)
            y = _dot(h_cur, w2_ref[...])
            if r + 1 < nsub:
                raw = _dot(lhs_next, w1_ref[...])
            if not first:
                y = y + acc_ref[rows(r), :]
            if last:
                o_ref[rows(r), :] = _layer_norm(ALPHA * x1_ref[rows(r), :] + y, g2_ref[...], b2_ref[...])
            else:
                acc_ref[rows(r), :] = y
            if r + 1 < nsub:
                h_next = act(raw)

    if nj == 1:
        chunk(True, True)
    else:
        pl.when(j == 0)(lambda: chunk(True, False))
        if nj > 2:
            pl.when((j > 0) & (j < nj - 1))(lambda: chunk(False, False))
        pl.when(j == nj - 1)(lambda: chunk(False, True))


def _tile_rows(n, want):
    t = min(n, want)
    assert n % t == 0, (n, t)
    return t


def _embed_ln(x, g, b):
    n, d = x.shape
    tm = _tile_rows(n, 512)
    return pl.pallas_call(
        _embed_ln_kernel, grid=(n // tm,), name="embed_ln",
        in_specs=[pl.BlockSpec((tm, d), lambda i: (i, 0)), pl.BlockSpec((1, d), lambda i: (0, 0)),
                  pl.BlockSpec((1, d), lambda i: (0, 0))],
        out_specs=pl.BlockSpec((tm, d), lambda i: (i, 0)),
        out_shape=jax.ShapeDtypeStruct((n, d), F32), compiler_params=_cparams(("parallel",)),
    )(x, g.reshape(1, d), b.reshape(1, d))


def _fold_fourier_weights(w_in, w_fnet, seq):
    c_bd, s_bd = _channel_dft_blockdiag(seq)
    depth = w_in.shape[0]
    sq = pl.BlockSpec((W_MIX, W_MIX), lambda l: (0, 0))
    return pl.pallas_call(
        _fold_kernel, grid=(depth,), name="fold_fourier_weights",
        in_specs=[pl.BlockSpec((None, D_MODEL, W_MIX), lambda l: (l, 0, 0)),
                  pl.BlockSpec((None, W_MIX, W_MIX), lambda l: (l, 0, 0)), sq, sq],
        out_specs=pl.BlockSpec((None, D_MODEL, 2 * W_MIX), lambda l: (l, 0, 0)),
        out_shape=jax.ShapeDtypeStruct((depth, D_MODEL, 2 * W_MIX), BF16),
        compiler_params=_cparams(("parallel",)),
    )(w_in, w_fnet, jnp.asarray(c_bd), jnp.asarray(s_bd))


def _inproj(x, wfold, w_in_bf, wvt_bf, cos, sin, gq, gk, ones, layer, batch, seq):
    n = x.shape[0]
    tm = _tile_rows(seq, INPROJ_ROWS)
    kvw = N_KV_GQA * HEAD_DIM
    pos_blocks = seq // tm
    nbr = len(DIL_CONFIGS)
    assert DIL_CONFIGS[0][1] == 1 and all(tm % (dil * 16) == 0 for _, dil in DIL_CONFIGS)

    def rows(c):
        return pl.BlockSpec((tm, c), lambda i: (i, 0))

    def whole(r, c):
        return pl.BlockSpec((r, c), lambda i: (0, 0))

    def flat(c, dt):
        return rows(c), jax.ShapeDtypeStruct((n, c), dt)

    dil_out = (pl.BlockSpec((None, None, nbr, tm, W_MIX), lambda i: (i // pos_blocks, i % pos_blocks, 0, 0, 0)),
               jax.ShapeDtypeStruct((batch, pos_blocks, nbr, tm, W_MIX), BF16))
    vt_out = (pl.BlockSpec((2 * kvw, tm), lambda i: (0, i)),
              jax.ShapeDtypeStruct((2 * kvw, n), BF16))
    outs = [flat(W_MIX, F32), flat(W_MIX, F32), flat(W_MIX, BF16), flat(kvw, BF16), vt_out, flat(2 * W_MIX, F32),
            dil_out, dil_out, dil_out]
    return pl.pallas_call(
        _inproj_kernel, grid=(n // tm,), name="inproj",
        in_specs=[rows(D_MODEL),
                  pl.BlockSpec((None, D_MODEL, 2 * W_MIX), lambda i: (layer, 0, 0)),
                  pl.BlockSpec((None, D_MODEL, D_IN), lambda i: (layer, 0, 0)),
                  pl.BlockSpec((None, kvw, D_MODEL), lambda i: (layer, 0, 0)),
                  pl.BlockSpec((tm, W_MIX), lambda i: (i % pos_blocks, 0)),
                  pl.BlockSpec((tm, W_MIX), lambda i: (i % pos_blocks, 0)),
                  pl.BlockSpec((None, 1, W_MIX), lambda i: (layer, 0, 0)),
                  pl.BlockSpec((None, 1, kvw), lambda i: (layer, 0, 0)),
                  whole(W_MIX, W_MIX)],
        out_specs=[o[0] for o in outs], out_shape=[o[1] for o in outs],
        scratch_shapes=[pltpu.VMEM((W_MIX // 128, tm, 128), F32)] * 3,
        compiler_params=_cparams(("parallel",)),
    )(x, wfold, w_in_bf, wvt_bf, cos, sin, gq, gk, ones)


def _fourier(zr, zi, g1, g2, twr, twi, batch, seq):
    a = seq // FFT_R
    blk = pl.BlockSpec((seq, 128), lambda b, j: (b, j))

    def whole(shape):
        return pl.BlockSpec(shape, lambda b, j: (0,) * len(shape))

    return pl.pallas_call(
        functools.partial(_fourier_kernel, a=a), grid=(batch, W_MIX // 128), name="fourier_mix",
        in_specs=[blk, blk, whole((2 * a, 2 * a)), whole((FFT_R, 2 * FFT_R)), whole((FFT_R, a, 128)),
                  whole((FFT_R, a, 128))],
        out_specs=blk, out_shape=jax.ShapeDtypeStruct((batch * seq, W_MIX), F32),
        scratch_shapes=[pltpu.VMEM((seq, 128), F32)] * 2,
        compiler_params=_cparams(("parallel", "parallel")),
    )(zr, zi, g1, g2, twr, twi)


def _gqa(qn, kn, v, batch, seq):
    tq = _tile_rows(seq, 512)
    tk = _tile_rows(seq, 256)
    nq = seq // tq
    kvw = N_KV_GQA * HEAD_DIM
    rep = (W_MIX // HEAD_DIM) // N_KV_GQA
    assert (seq // tk) % 2 == 0
    return pl.pallas_call(
        functools.partial(_gqa_kernel, tq=tq, tk=tk, seq=seq), grid=(batch, nq), name="gqa_attention",
        in_specs=[pl.BlockSpec((tq, W_MIX), lambda b, i: (b * nq + i, 0)),
                  pl.BlockSpec((seq, kvw), lambda b, i: (b, 0)),
                  pl.BlockSpec((2 * kvw, seq), lambda b, i: (0, b))],
        out_specs=pl.BlockSpec((tq, W_MIX), lambda b, i: (b * nq + i, 0)),
        out_shape=jax.ShapeDtypeStruct((batch * seq, W_MIX), BF16),
        scratch_shapes=[pltpu.VMEM((N_KV_GQA * rep * tq // GQA_UNIT_LANES, tk, GQA_UNIT_LANES), F32)] * 2,
        compiler_params=_cparams(("parallel", "parallel")),
    )(qn, kn, v)


def _conv(uc, dw, cb, g, b, wpw_bf, layer, seq):
    n = uc.shape[0]
    ts = _tile_rows(seq, 512)
    nblk = seq // ts
    hb = ts // HALO
    last = n // HALO - 1

    def vec():
        return pl.BlockSpec((None, 1, W_MIX), lambda i: (layer, 0, 0))

    return pl.pallas_call(
        functools.partial(_conv_kernel, ts=ts, nblk=nblk), grid=(n // ts,), name="conformer_conv",
        in_specs=[pl.BlockSpec((ts, 2 * W_MIX), lambda i: (i, 0)),
                  pl.BlockSpec((HALO, 2 * W_MIX), lambda i: (jnp.maximum(i * hb - 1, 0), 0)),
                  pl.BlockSpec((HALO, 2 * W_MIX), lambda i: (jnp.minimum((i + 1) * hb, last), 0)),
                  pl.BlockSpec((None, CONV_WIDTH, W_MIX), lambda i: (layer, 0, 0)),
                  vec(), vec(), vec(),
                  pl.BlockSpec((None, W_MIX, W_MIX), lambda i: (layer, 0, 0))],
        out_specs=pl.BlockSpec((ts, W_MIX), lambda i: (i, 0)),
        out_shape=jax.ShapeDtypeStruct((n, W_MIX), BF16),
        scratch_shapes=[pltpu.VMEM((ts + 2 * HALO, W_MIX), F32), pltpu.VMEM((7, ts + 2 * HALO, W_MIX), F32)],
        compiler_params=_cparams(("parallel",)),
    )(uc, uc, uc, dw, cb, g, b, wpw_bf)


def _bias_tiles(rel_bias, tq):
    idx = jnp.asarray(_bucket_tiles(tq))
    nbr, _, win = idx.shape
    heads = W_MIX // HEAD_DIM
    return pl.pallas_call(
        _bias_kernel, grid=(nbr,), name="dilated_bias_tiles",
        in_specs=[pl.BlockSpec((None, tq, win), lambda r: (r, 0, 0)),
                  pl.BlockSpec(memory_space=pltpu.SMEM)],
        out_specs=pl.BlockSpec((None, heads, tq, win), lambda r: (r, 0, 0, 0)),
        out_shape=jax.ShapeDtypeStruct((nbr, heads, tq, win), F32),
        compiler_params=_cparams(("parallel",)),
    )(idx, rel_bias)


def _dilated(qd, kd, vd, bias, tq, batch, seq):
    heads = W_MIX // HEAD_DIM
    win = tq + 2 * DIL_N
    tiles, nbr, tm = qd.shape[1], qd.shape[2], qd.shape[3]
    max_dil = max(dil for _, dil in DIL_CONFIGS)
    pad_rows = seq + max_dil * 2 * DIL_N
    assert (seq // tq) % 2 == 0
    blk = pl.BlockSpec((None, tiles, None, tm, W_MIX), lambda b, r: (b, 0, r, 0, 0))
    return pl.pallas_call(
        functools.partial(_dilated_kernel, seq=seq, tm=tm, tq=tq), grid=(batch, nbr), name="dilated_attention",
        in_specs=[blk, blk, blk, pl.BlockSpec((None, heads, tq, win), lambda b, r: (r, 0, 0, 0))],
        out_specs=pl.BlockSpec((seq, W_MIX), lambda b, r: (b, 0)),
        out_shape=jax.ShapeDtypeStruct((batch * seq, W_MIX), BF16),
        scratch_shapes=[pltpu.VMEM((seq, W_MIX), BF16), pltpu.VMEM((pad_rows, W_MIX), BF16),
                        pltpu.VMEM((pad_rows, 2 * W_MIX), BF16),
                        pltpu.VMEM((heads, seq, 2 * HEAD_DIM), F32), pltpu.VMEM((heads, seq, 2 * HEAD_DIM), F32),
                        pltpu.VMEM((heads * tq, win), F32), pltpu.VMEM((heads * tq, win), F32)],
        compiler_params=_cparams(("parallel", "arbitrary")),
    )(qd, kd, vd, bias)


def _mix_ffn(x, ya, yb, yc, yd, w_out_bf, g1, b1, w1_bf, w2_bf, g2, b2, layer):
    n = x.shape[0]
    tm = _tile_rows(n, 1024)
    tf = FFN_CHUNK

    def rows(c):
        return pl.BlockSpec((tm, c), lambda i, j: (i, 0))

    def vec():
        return pl.BlockSpec((None, 1, D_MODEL), lambda i, j: (layer, 0, 0))

    return pl.pallas_call(
        functools.partial(_mix_ffn_kernel, nj=D_FF // tf), grid=(n // tm, D_FF // tf), name="outproj_ffn",
        in_specs=[rows(D_MODEL)] + [rows(W_MIX)] * 4
                 + [pl.BlockSpec((None, D_MODEL, D_MODEL), lambda i, j: (layer, 0, 0)), vec(), vec(),
                    pl.BlockSpec((None, D_MODEL, tf), lambda i, j: (layer, 0, j)),
                    pl.BlockSpec((None, tf, D_MODEL), lambda i, j: (layer, j, 0)), vec(), vec()],
        out_specs=rows(D_MODEL), out_shape=jax.ShapeDtypeStruct((n, D_MODEL), F32),
        scratch_shapes=[pltpu.VMEM((tm, D_MODEL), F32), pltpu.VMEM((tm, D_MODEL), F32)],
        compiler_params=_cparams(("parallel", "arbitrary")),
    )(x, ya, yb, yc, yd, w_out_bf, g1, b1, w1_bf, w2_bf, g2, b2)


def kernel(x, emb_ln_g, emb_ln_b, w_in, w_fnet, q_norm_g, k_norm_g, conv_dw, conv_b, conv_ln_g, conv_ln_b,
           w_conv_out, w_out, ln1_g, ln1_b, w_ff1, w_ff2, ln2_g, ln2_b, rel_bias):
    batch, seq, d = x.shape
    depth = w_in.shape[0]
    assert d == D_MODEL and seq % (FFT_R * 16) == 0 and seq % (DIL_CONFIGS[-1][1] * DIL_N) == 0
    n = batch * seq
    a = seq // FFT_R
    heads = W_MIX // HEAD_DIM
    kvw = N_KV_GQA * HEAD_DIM

    cos_np, sin_np = _rope_tables(seq)
    cos, sin = jnp.asarray(cos_np), jnp.asarray(sin_np)
    ones = jnp.asarray(np.kron(np.eye(heads), np.ones((HEAD_DIM, HEAD_DIM))), BF16)
    g1_np, g2_np = _fft_stage_mats(a)
    g1, g2 = jnp.asarray(g1_np, BF16), jnp.asarray(g2_np, BF16)
    twr_np, twi_np = _twiddle(a, seq)
    twr = jnp.broadcast_to(jnp.asarray(twr_np.T)[:, :, None], (FFT_R, a, 128))
    twi = jnp.broadcast_to(jnp.asarray(twi_np.T)[:, :, None], (FFT_R, a, 128))

    w_in_bf = w_in.astype(BF16)
    wvt_bf = jnp.swapaxes(w_in[:, :, COL_KVB + kvw:COL_GLU], 1, 2).astype(BF16)
    w_out_bf = w_out.astype(BF16)
    w_ff1_bf = w_ff1.astype(BF16)
    w_ff2_bf = w_ff2.astype(BF16)
    wpw_bf = w_conv_out.astype(BF16)
    gq = jnp.tile(q_norm_g, (1, heads)).reshape(depth, 1, W_MIX)
    gk = jnp.tile(k_norm_g, (1, N_KV_GQA)).reshape(depth, 1, kvw)
    r3 = lambda t: t.reshape(depth, 1, t.shape[-1])

    wfold = _fold_fourier_weights(w_in, w_fnet, seq)
    tq_d = min(DIL_TQ, seq // MAX_DIL)
    bias = _bias_tiles(rel_bias, tq_d)

    h = _embed_ln(x.reshape(n, d), emb_ln_g, emb_ln_b)
    for l in range(depth):
        zr, zi, qn, kn, v, uc, qd, kd, vd = _inproj(h, wfold, w_in_bf, wvt_bf, cos, sin, gq, gk, ones, l, batch, seq)
        ya = _fourier(zr, zi, g1, g2, twr, twi, batch, seq)
        yb = _gqa(qn, kn, v, batch, seq)
        yc = _conv(uc, conv_dw, r3(conv_b), r3(conv_ln_g), r3(conv_ln_b), wpw_bf, l, seq)
        yd = _dilated(qd, kd, vd, bias, tq_d, batch, seq)
        h = _mix_ffn(h, ya, yb, yc, yd, w_out_bf, r3(ln1_g), r3(ln1_b), w_ff1_bf, w_ff2_bf, r3(ln2_g), r3(ln2_b), l)
    return h.reshape(batch, seq, d)
```

```python
import functools
import math

import numpy as np
import jax
import jax.numpy as jnp
from jax import lax
from jax.experimental import pallas as pl
from jax.experimental.pallas import tpu as pltpu

F32 = jnp.float32
BF16 = jnp.bfloat16

D_MODEL = 1024
DEPTH = 4
HEAD_DIM = 64
W_MIX = 256
N_KV_GQA = 2
CONV_WIDTH = 31
CONV_PAD = CONV_WIDTH // 2
DIL_CONFIGS = ((128, 1), (512, 4), (2048, 16))
DIL_N = 64
MAX_DIL = 16
DIL_TQ = 128
D_FF = 4 * D_MODEL
GRID_W = 64
ROPE_THETA = 10000.0
REL_BUCKETS = 32
REL_MAX_DIST = 1024
ALPHA = (2 * DEPTH) ** 0.25
LN_EPS = 1e-5
RMS_EPS = 1e-6
NEG = -1e30
ATTN_SCALE = HEAD_DIM ** -0.5
LOG2E = math.log2(math.e)

COL_QB, COL_KVB, COL_GLU, COL_QD, COL_KD, COL_VD, D_IN = 256, 512, 768, 1280, 1536, 1792, 2048

FFT_R = 64
INPROJ_ROWS = 1024
GQA_UNIT_LANES = 256
FFN_CHUNK = 2048
FFN_SUB_ROWS = 256
HALO = 16
VMEM_LIMIT = 56 * 1024 * 1024


def _cparams(sem):
    return pltpu.CompilerParams(dimension_semantics=sem, vmem_limit_bytes=VMEM_LIMIT)


def _dft_cos_sin(n):
    j = np.arange(n)
    ang = 2.0 * np.pi * ((j[:, None] * j[None, :]) % n) / n
    return np.cos(ang), np.sin(ang)


def _channel_dft_blockdiag(seq):
    c, s = _dft_cos_sin(HEAD_DIM)
    scale = 1.0 / math.sqrt(seq * HEAD_DIM)
    eye = np.eye(W_MIX // HEAD_DIM)
    return (np.kron(eye, c) * scale).astype(np.float32), (np.kron(eye, -s) * scale).astype(np.float32)


def _fft_stage_mats(a):
    ca, sa = _dft_cos_sin(a)
    g1 = np.block([[ca, sa], [-sa, ca]])
    c2, s2 = _dft_cos_sin(FFT_R)
    g2 = np.concatenate([c2, s2], axis=1)
    return g1.astype(np.float32), g2.astype(np.float32)


def _twiddle(a, seq):
    c = np.arange(a)[:, None]
    b = np.arange(FFT_R)[None, :]
    ang = 2.0 * np.pi * ((b * c) % seq) / seq
    return np.cos(ang).astype(np.float32), (-np.sin(ang)).astype(np.float32)


def _rope_tables(seq):
    rows = seq // GRID_W
    row = np.repeat(np.arange(rows), GRID_W).astype(np.float32)
    col = np.tile(np.arange(GRID_W), rows).astype(np.float32)
    nf = HEAD_DIM // 4
    inv = (ROPE_THETA ** (-np.arange(nf, dtype=np.float32) / nf)).astype(np.float32)
    ar = row[:, None] * inv
    ac = col[:, None] * inv
    cos = np.concatenate([np.cos(ar), np.cos(ar), np.cos(ac), np.cos(ac)], -1)
    sin = np.concatenate([-np.sin(ar), np.sin(ar), -np.sin(ac), np.sin(ac)], -1)
    reps = W_MIX // HEAD_DIM
    return np.tile(cos, (1, reps)).astype(np.float32), np.tile(sin, (1, reps)).astype(np.float32)


def _t5_bucket_np(rel):
    nb = REL_BUCKETS // 2
    max_exact = nb // 2
    ret = np.where(rel > 0, nb, 0)
    n = np.abs(rel)
    nf = np.maximum(n, 1).astype(np.float32)
    large = max_exact + (np.log(nf / np.float32(max_exact)) / np.float32(math.log(REL_MAX_DIST / max_exact))
                         * np.float32(nb - max_exact)).astype(np.int32)
    large = np.minimum(large, nb - 1)
    return ret + np.where(n < max_exact, n, large)


def _bucket_tiles(tq):
    qi = np.arange(tq)[:, None]
    ki = np.arange(tq + 2 * DIL_N)[None, :]
    rel = ki - DIL_N - qi
    tiles = []
    for _, dil in DIL_CONFIGS:
        tiles.append(np.where(np.abs(rel) <= DIL_N, _t5_bucket_np(rel * dil), -1))
    return np.stack(tiles).astype(np.int32)


def _layer_norm(r, g, b):
    mu = jnp.mean(r, axis=-1, keepdims=True)
    d = r - mu
    var = jnp.mean(d * d, axis=-1, keepdims=True)
    return d * lax.rsqrt(var + LN_EPS) * g + b


def _dot(a, b):
    return jnp.dot(a, b, preferred_element_type=F32)


def _dot_nt(a, b):
    return lax.dot_general(a, b, (((1,), (1,)), ((), ())), preferred_element_type=F32)


def _norm_rope(x, g, cos, sin, ones):
    x2 = x * x
    hi = x2.astype(BF16)
    lo = (x2 - hi.astype(F32)).astype(BF16)
    ss = _dot(hi, ones) + _dot(lo, ones)
    xn = x * lax.rsqrt(ss * (1.0 / HEAD_DIM) + RMS_EPS) * g
    w = x.shape[1]
    lane = lax.broadcasted_iota(jnp.int32, x.shape, 1)
    quarter = HEAD_DIM // 4
    partner = jnp.where((lane % (2 * quarter)) < quarter,
                        pltpu.roll(xn, w - quarter, 1), pltpu.roll(xn, quarter, 1))
    return xn * cos + partner * sin


def _embed_ln_kernel(x_ref, g_ref, b_ref, o_ref):
    o_ref[...] = _layer_norm(x_ref[...], g_ref[...], b_ref[...])


def _fold_kernel(win_ref, wv_ref, wf_ref, c_ref, s_ref, o_ref, wvt_ref):
    wvt_ref[...] = wv_ref[...].T.astype(BF16)
    hp = lax.Precision.HIGHEST
    wf = wf_ref[...]
    mc = jnp.dot(c_ref[...], wf, precision=hp, preferred_element_type=F32)
    ms = jnp.dot(s_ref[...], wf, precision=hp, preferred_element_type=F32)
    win = win_ref[...]
    o_ref[:, :W_MIX] = jnp.dot(win, mc, precision=hp, preferred_element_type=F32).astype(BF16)
    o_ref[:, W_MIX:] = jnp.dot(win, ms, precision=hp, preferred_element_type=F32).astype(BF16)


def _inproj_kernel(x_ref, wfold_ref, w_ref, wvt_ref, cos_ref, sin_ref, gq_ref, gk_ref, ones_ref,
                   zr_ref, zi_ref, qn_ref, kn_ref, vt_ref, uc_ref, qd_ref, kd_ref, vd_ref, stage_q, stage_k, stage_v):
    xb = x_ref[...].astype(BF16)
    kvw = N_KV_GQA * HEAD_DIM
    tm = xb.shape[0]
    q = _dot(xb, w_ref[:, COL_QB:COL_KVB])
    k = _dot(xb, w_ref[:, COL_KVB:COL_KVB + kvw])
    z = _dot(xb, wfold_ref[...])
    vt = _dot_nt(wvt_ref[...], xb)
    dil_u = [_dot(xb, w_ref[:, col:col + W_MIX]) for col in (COL_QD, COL_KD, COL_VD)]
    uc_ref[...] = _dot(xb, w_ref[:, COL_GLU:COL_QD])

    zr_ref[...] = z[:, :W_MIX]
    zi_ref[...] = z[:, W_MIX:]
    row = lax.broadcasted_iota(jnp.int32, (HEAD_DIM, tm), 0)
    one_row = jnp.where(row == 0, 1.0, 0.0).astype(BF16)
    vt_ref[...] = jnp.concatenate(
        [piece for g in range(N_KV_GQA)
         for piece in (vt[g * HEAD_DIM:(g + 1) * HEAD_DIM].astype(BF16), one_row)], axis=0)

    halves = W_MIX // 128
    for u, is_q, out_ref, stage in zip(dil_u, (True, False, False), (qd_ref, kd_ref, vd_ref),
                                       (stage_q, stage_k, stage_v)):
        if is_q:
            u = u * (ATTN_SCALE * LOG2E)
        for half in range(halves):
            stage[half] = u[:, half * 128:(half + 1) * 128]
        for br, (_, dil) in enumerate(DIL_CONFIGS):
            if dil == 1:
                out_ref[br] = u.astype(BF16)
                continue
            c = tm // dil
            for r in range(dil):
                out_ref[br, r * c:(r + 1) * c, :] = jnp.concatenate(
                    [stage[half, pl.ds(r, c, stride=dil), :] for half in range(halves)], axis=1).astype(BF16)

    cos = cos_ref[...]
    sin = sin_ref[...]
    ones = ones_ref[...]
    qn_ref[...] = (_norm_rope(q, gq_ref[...], cos, sin, ones) * (ATTN_SCALE * LOG2E)).astype(BF16)
    kn_ref[...] = _norm_rope(k, gk_ref[...], cos[:, :kvw], sin[:, :kvw], ones[:kvw, :kvw]).astype(BF16)


def _fourier_kernel(zr_ref, zi_ref, g1_ref, g2_ref, twr_ref, twi_ref, o_ref, yr_scr, yi_scr, *, a):
    g1 = g1_ref[...]
    g2 = g2_ref[...]

    def stage1(b, carry):
        z = jnp.concatenate([zr_ref[pl.ds(b, a, stride=FFT_R), :], zi_ref[pl.ds(b, a, stride=FFT_R), :]], axis=0)
        y = _dot(g1, z.astype(BF16))
        yr, yi = y[:a], y[a:]
        twr = twr_ref[b]
        twi = twi_ref[b]
        row0 = pl.multiple_of(b * a, a)
        yr_scr[pl.ds(row0, a), :] = yr * twr - yi * twi
        yi_scr[pl.ds(row0, a), :] = yr * twi + yi * twr
        return carry

    lax.fori_loop(0, FFT_R, stage1, 0, unroll=8)

    def stage2(c, carry):
        y = jnp.concatenate([yr_scr[pl.ds(c, FFT_R, stride=a), :], yi_scr[pl.ds(c, FFT_R, stride=a), :]], axis=0)
        o_ref[pl.ds(c, FFT_R, stride=a), :] = _dot(g2, y.astype(BF16))
        return carry

    lax.fori_loop(0, a, stage2, 0, unroll=8)


def _gqa_kernel(q_ref, k_ref, vt_ref, o_ref, st_a, st_b, *, tq, tk, seq):
    nk = seq // tk
    rep = (W_MIX // HEAD_DIM) // N_KV_GQA
    zeros = jnp.zeros((tq, HEAD_DIM), BF16)
    q2 = []
    for g in range(N_KV_GQA):
        parts = []
        for r in range(rep):
            h = g * rep + r
            qh = q_ref[:, h * HEAD_DIM:(h + 1) * HEAD_DIM]
            parts.append(jnp.concatenate([zeros] * g + [qh] + [zeros] * (N_KV_GQA - 1 - g), axis=1))
        q2.append(jnp.concatenate(parts, axis=0))
    rows = rep * tq
    vw = 2 * HEAD_DIM

    lanes = GQA_UNIT_LANES
    units = [(g, u) for g in range(N_KV_GQA) for u in range(rows // lanes)]
    q_unit = [q2[g][u * lanes:(u + 1) * lanes, :] for g, u in units]

    def scores(c, st_scr, i):
        off = pl.multiple_of(c * tk, tk)
        st_scr[i] = _dot_nt(k_ref[pl.ds(off, tk), :], q_unit[i])

    def softmax_pv(c, st_scr, i, state):
        off = pl.multiple_of(c * tk, tk)
        g = units[i][0]
        m, acc = state
        st = st_scr[i]
        vt = vt_ref[g * vw:(g + 1) * vw, pl.ds(off, tk)]
        m_new = jnp.maximum(m, jnp.max(st, axis=0, keepdims=True))
        alpha = jnp.exp2(m - m_new)
        pt = jnp.exp2(st - m_new).astype(BF16)
        return m_new, alpha * acc + _dot(vt, pt)

    def half(c_cur, c_next, st_cur, st_next, carry):
        out = []
        for i in range(len(units)):
            scores(c_next, st_next, i)
            out.append(softmax_pv(c_cur, st_cur, i, carry[i]))
        return tuple(out)

    def body(c2, carry):
        c = 2 * c2
        carry = half(c, c + 1, st_a, st_b, carry)
        return half(c + 1, (c + 2) % nk, st_b, st_a, carry)

    init = tuple((jnp.full((1, lanes), -jnp.inf, F32), jnp.zeros((vw, lanes), F32)) for _ in units)
    for i in range(len(units)):
        scores(0, st_a, i)
    final = lax.fori_loop(0, nk // 2, body, init)
    for g in range(N_KV_GQA):
        acc = jnp.concatenate([final[i][1] for i, (gi, _) in enumerate(units) if gi == g], axis=1)
        o = (acc[:HEAD_DIM] / acc[HEAD_DIM:HEAD_DIM + 1]).T
        for r in range(rep):
            h = g * rep + r
            o_ref[:, h * HEAD_DIM:(h + 1) * HEAD_DIM] = o[r * tq:(r + 1) * tq].astype(BF16)


def _conv_kernel(cur_ref, prev_ref, next_ref, dw_ref, cb_ref, g_ref, b_ref, wpw_ref, o_ref, hpad, shifted, *, ts, nblk):
    c = pl.program_id(0) % nblk

    def glu(u):
        return u[:, :W_MIX] * jax.nn.sigmoid(u[:, W_MIX:])

    hpad[0:HALO, :] = jnp.where(c > 0, glu(prev_ref[...]), 0.0)
    hpad[HALO:HALO + ts, :] = glu(cur_ref[...])
    hpad[HALO + ts:HALO + ts + HALO, :] = jnp.where(c < nblk - 1, glu(next_ref[...]), 0.0)
    sublanes = 8
    first = HALO - CONV_PAD
    span = ts + ((first + CONV_WIDTH - 1) // sublanes) * sublanes
    for b in range(1, sublanes):
        shifted[b - 1, 0:span, :] = hpad[b:b + span, :]
    acc = jnp.zeros((ts, W_MIX), F32)
    for j in range(CONV_WIDTH):
        start = first + j
        base = (start // sublanes) * sublanes
        b = start % sublanes
        tap = hpad[base:base + ts, :] if b == 0 else shifted[b - 1, base:base + ts, :]
        acc = acc + dw_ref[j:j + 1, :] * tap
    h = _layer_norm(acc + cb_ref[...], g_ref[...], b_ref[...])
    h = h * jax.nn.sigmoid(h)
    o_ref[...] = _dot(h.astype(BF16), wpw_ref[...]).astype(BF16)


def _bias_kernel(idx_ref, rb_ref, o_ref):
    idx = idx_ref[...]
    for h in range(W_MIX // HEAD_DIM):
        tile = jnp.full(idx.shape, NEG, F32)
        for b in range(REL_BUCKETS):
            tile = jnp.where(idx == b, rb_ref[b, h] * LOG2E, tile)
        o_ref[h] = tile


def _dilated_branch(q_in, k_in, v_in, bias_ref, qbuf, kbuf, vbuf, m_st, acc_st, s_a, s_b, *, dil, first, seq, tm, tq):
    n = DIL_N
    heads = W_MIX // HEAD_DIM
    length = seq // dil
    lp = length + 2 * n
    c = tm // dil
    win = tq + 2 * n
    nb = length // tq
    nblocks = dil * nb
    slab = 2 * HEAD_DIM

    lane = lax.broadcasted_iota(jnp.int32, (c, HEAD_DIM), 1)
    one_col = jnp.where(lane == 0, 1.0, 0.0).astype(BF16)
    for r in range(dil):
        base = r * lp
        for buf in (kbuf, vbuf):
            zeros = jnp.zeros((n, buf.shape[1]), BF16)
            buf[base:base + n, :] = zeros
            buf[base + n + length:base + lp, :] = zeros
        for t in range(seq // tm):
            src = slice(r * c, (r + 1) * c)
            qbuf[r * length + t * c:r * length + (t + 1) * c, :] = q_in[t, src, :]
            kbuf[base + n + t * c:base + n + (t + 1) * c, :] = k_in[t, src, :]
            vv = v_in[t, src, :]
            vbuf[base + n + t * c:base + n + (t + 1) * c, :] = jnp.concatenate(
                [piece for h in range(heads) for piece in (vv[:, h * HEAD_DIM:(h + 1) * HEAD_DIM], one_col)], axis=1)

    head_of_lane = lax.broadcasted_iota(jnp.int32, (tq, W_MIX), 1) // HEAD_DIM

    def block_pos(j):
        r = j // nb
        i = j - r * nb
        return r, i, pl.multiple_of(j * tq + r * 2 * n, 2 * n if tq % (2 * n) == 0 else tq)

    def scores(j, s_scr):
        _, _, krow = block_pos(j)
        qb = qbuf[pl.ds(pl.multiple_of(j * tq, tq), tq), :]
        kw = kbuf[pl.ds(krow, win), :]
        qs = jnp.concatenate([jnp.where(head_of_lane == h, qb, jnp.zeros_like(qb)) for h in range(heads)], axis=0)
        s_scr[...] = _dot_nt(qs, kw)

    def update(j, s_scr):
        r, i, krow = block_pos(j)
        vw = vbuf[pl.ds(krow, win), :]
        kpos = i * tq - n + lax.broadcasted_iota(jnp.int32, (tq, win), 1)
        valid = (kpos >= 0) & (kpos < length)
        rows = pl.ds(r + dil * i * tq, tq, stride=dil) if dil > 1 else pl.ds(pl.multiple_of(i * tq, tq), tq)
        for h in range(heads):
            s = jnp.where(valid, s_scr[h * tq:(h + 1) * tq, :] + bias_ref[h], NEG)
            mblk = jnp.max(s, axis=-1, keepdims=True)
            if first:
                m_new = jnp.broadcast_to(mblk, (tq, slab))
            else:
                m_old = m_st[h, rows, :]
                m_new = jnp.maximum(m_old, mblk)
            m_row = jnp.tile(m_new, (1, win // slab)) if win % slab == 0 else m_new[:, :1]
            p = jnp.exp2(s - m_row).astype(BF16)
            pv = _dot(p, vw[:, h * slab:(h + 1) * slab])
            if not first:
                pv = jnp.exp2(m_old - m_new) * acc_st[h, rows, :] + pv
            m_st[h, rows, :] = m_new
            acc_st[h, rows, :] = pv

    def body(j2, carry):
        j = 2 * j2
        scores(j + 1, s_b)
        update(j, s_a)
        scores((j + 2) % nblocks, s_a)
        update(j + 1, s_b)
        return carry

    scores(0, s_a)
    lax.fori_loop(0, nblocks // 2, body, 0)


def _dilated_kernel(q_in, k_in, v_in, bias_ref, o_ref, qbuf, kbuf, vbuf, m_st, acc_st, s_a, s_b, *, seq, tm, tq):
    br = pl.program_id(1)
    for idx, (_, dil) in enumerate(DIL_CONFIGS):
        @pl.when(br == idx)
        def _(idx=idx, dil=dil):
            _dilated_branch(q_in, k_in, v_in, bias_ref, qbuf, kbuf, vbuf, m_st, acc_st, s_a, s_b,
                            dil=dil, first=(idx == 0), seq=seq, tm=tm, tq=tq)

    @pl.when(br == len(DIL_CONFIGS) - 1)
    def _():
        heads = W_MIX // HEAD_DIM
        rows_per = 256

        def finish(t, carry):
            rows = pl.ds(pl.multiple_of(t * rows_per, rows_per), rows_per)
            outs = []
            for h in range(heads):
                a = acc_st[h, rows, :]
                outs.append(a[:, :HEAD_DIM] / a[:, HEAD_DIM:HEAD_DIM + 1])
            o_ref[rows, :] = jnp.concatenate(outs, axis=1).astype(BF16)
            return carry

        lax.fori_loop(0, seq // rows_per, finish, 0)


def _mix_ffn_kernel(x_ref, ya_ref, yb_ref, yc_ref, yd_ref, wo_ref, g1_ref, b1_ref, w1_ref, w2_ref, g2_ref, b2_ref,
                    o_ref, x1_ref, acc_ref, *, nj):
    j = pl.program_id(1)
    sub = FFN_SUB_ROWS
    nsub = x_ref.shape[0] // sub

    def rows(r):
        return slice(r * sub, (r + 1) * sub)

    def mix(r):
        y_in = jnp.concatenate([ya_ref[rows(r), :].astype(BF16), yb_ref[rows(r), :], yc_ref[rows(r), :],
                                yd_ref[rows(r), :]], axis=1)
        x1 = _layer_norm(ALPHA * x_ref[rows(r), :] + _dot(y_in, wo_ref[...]), g1_ref[...], b1_ref[...])
        x1_ref[rows(r), :] = x1
        return x1.astype(BF16)

    def act(h):
        return jnp.square(jnp.maximum(h, 0.0)).astype(BF16)

    def chunk(first, last):
        def lhs(r):
            return mix(r) if first else x1_ref[rows(r), :].astype(BF16)

        h_next = act(_dot(lhs(0), w1_ref[...]))
        for r in range(nsub):
            h_cur = h_next
            if r + 1 < nsub:
                lhs_next = lhs(r + 1)
            y = _dot(h_cur, w2_ref[...])
            if r + 1 < nsub:
                raw = _dot(lhs_next, w1_ref[...])
            if not first:
                y = y + acc_ref[rows(r), :]
            if last:
                o_ref[rows(r), :] = _layer_norm(ALPHA * x1_ref[rows(r), :] + y, g2_ref[...], b2_ref[...])
            else:
                acc_ref[rows(r), :] = y
            if r + 1 < nsub:
                h_next = act(raw)

    if nj == 1:
        chunk(True, True)
    else:
        pl.when(j == 0)(lambda: chunk(True, False))
        if nj > 2:
            pl.when((j > 0) & (j < nj - 1))(lambda: chunk(False, False))
        pl.when(j == nj - 1)(lambda: chunk(False, True))


def _tile_rows(n, want):
    t = min(n, want)
    assert n % t == 0, (n, t)
    return t


def _embed_ln(x, g, b):
    n, d = x.shape
    tm = _tile_rows(n, 512)
    return pl.pallas_call(
        _embed_ln_kernel, grid=(n // tm,), name="embed_ln",
        in_specs=[pl.BlockSpec((tm, d), lambda i: (i, 0)), pl.BlockSpec((1, d), lambda i: (0, 0)),
                  pl.BlockSpec((1, d), lambda i: (0, 0))],
        out_specs=pl.BlockSpec((tm, d), lambda i: (i, 0)),
        out_shape=jax.ShapeDtypeStruct((n, d), F32), compiler_params=_cparams(("parallel",)),
    )(x, g.reshape(1, d), b.reshape(1, d))


def _fold_fourier_weights(w_in, w_fnet, seq):
    c_bd, s_bd = _channel_dft_blockdiag(seq)
    depth = w_in.shape[0]
    sq = pl.BlockSpec((W_MIX, W_MIX), lambda l: (0, 0))
    kvw = N_KV_GQA * HEAD_DIM
    v_col = COL_KVB + kvw
    assert v_col % kvw == 0
    return pl.pallas_call(
        _fold_kernel, grid=(depth,), name="fold_weights",
        in_specs=[pl.BlockSpec((None, D_MODEL, W_MIX), lambda l: (l, 0, 0)),
                  pl.BlockSpec((None, D_MODEL, kvw), lambda l: (l, 0, v_col // kvw)),
                  pl.BlockSpec((None, W_MIX, W_MIX), lambda l: (l, 0, 0)), sq, sq],
        out_specs=[pl.BlockSpec((None, D_MODEL, 2 * W_MIX), lambda l: (l, 0, 0)),
                   pl.BlockSpec((None, kvw, D_MODEL), lambda l: (l, 0, 0))],
        out_shape=[jax.ShapeDtypeStruct((depth, D_MODEL, 2 * W_MIX), BF16),
                   jax.ShapeDtypeStruct((depth, kvw, D_MODEL), BF16)],
        compiler_params=_cparams(("parallel",)),
    )(w_in, w_in, w_fnet, jnp.asarray(c_bd), jnp.asarray(s_bd))


def _inproj(x, wfold, w_in_bf, wvt_bf, cos, sin, gq, gk, ones, layer, batch, seq):
    n = x.shape[0]
    tm = _tile_rows(seq, INPROJ_ROWS)
    kvw = N_KV_GQA * HEAD_DIM
    pos_blocks = seq // tm
    nbr = len(DIL_CONFIGS)
    assert DIL_CONFIGS[0][1] == 1 and all(tm % (dil * 16) == 0 for _, dil in DIL_CONFIGS)

    def rows(c):
        return pl.BlockSpec((tm, c), lambda i: (i, 0))

    def whole(r, c):
        return pl.BlockSpec((r, c), lambda i: (0, 0))

    def flat(c, dt):
        return rows(c), jax.ShapeDtypeStruct((n, c), dt)

    dil_out = (pl.BlockSpec((None, None, nbr, tm, W_MIX), lambda i: (i // pos_blocks, i % pos_blocks, 0, 0, 0)),
               jax.ShapeDtypeStruct((batch, pos_blocks, nbr, tm, W_MIX), BF16))
    vt_out = (pl.BlockSpec((2 * kvw, tm), lambda i: (0, i)),
              jax.ShapeDtypeStruct((2 * kvw, n), BF16))
    outs = [flat(W_MIX, F32), flat(W_MIX, F32), flat(W_MIX, BF16), flat(kvw, BF16), vt_out, flat(2 * W_MIX, F32),
            dil_out, dil_out, dil_out]
    return pl.pallas_call(
        _inproj_kernel, grid=(n // tm,), name="inproj",
        in_specs=[rows(D_MODEL),
                  pl.BlockSpec((None, D_MODEL, 2 * W_MIX), lambda i: (layer, 0, 0)),
                  pl.BlockSpec((None, D_MODEL, D_IN), lambda i: (layer, 0, 0)),
                  pl.BlockSpec((None, kvw, D_MODEL), lambda i: (layer, 0, 0)),
                  pl.BlockSpec((tm, W_MIX), lambda i: (i % pos_blocks, 0)),
                  pl.BlockSpec((tm, W_MIX), lambda i: (i % pos_blocks, 0)),
                  pl.BlockSpec((None, 1, W_MIX), lambda i: (layer, 0, 0)),
                  pl.BlockSpec((None, 1, kvw), lambda i: (layer, 0, 0)),
                  whole(W_MIX, W_MIX)],
        out_specs=[o[0] for o in outs], out_shape=[o[1] for o in outs],
        scratch_shapes=[pltpu.VMEM((W_MIX // 128, tm, 128), F32)] * 3,
        compiler_params=_cparams(("parallel",)),
    )(x, wfold, w_in_bf, wvt_bf, cos, sin, gq, gk, ones)


def _fourier(zr, zi, g1, g2, twr, twi, batch, seq):
    a = seq // FFT_R
    blk = pl.BlockSpec((seq, 128), lambda b, j: (b, j))

    def whole(shape):
        return pl.BlockSpec(shape, lambda b, j: (0,) * len(shape))

    return pl.pallas_call(
        functools.partial(_fourier_kernel, a=a), grid=(batch, W_MIX // 128), name="fourier_mix",
        in_specs=[blk, blk, whole((2 * a, 2 * a)), whole((FFT_R, 2 * FFT_R)), whole((FFT_R, a, 128)),
                  whole((FFT_R, a, 128))],
        out_specs=blk, out_shape=jax.ShapeDtypeStruct((batch * seq, W_MIX), F32),
        scratch_shapes=[pltpu.VMEM((seq, 128), F32)] * 2,
        compiler_params=_cparams(("parallel", "parallel")),
    )(zr, zi, g1, g2, twr, twi)


def _gqa(qn, kn, v, batch, seq):
    tq = _tile_rows(seq, 512)
    tk = _tile_rows(seq, 256)
    nq = seq // tq
    kvw = N_KV_GQA * HEAD_DIM
    rep = (W_MIX // HEAD_DIM) // N_KV_GQA
    assert (seq // tk) % 2 == 0
    return pl.pallas_call(
        functools.partial(_gqa_kernel, tq=tq, tk=tk, seq=seq), grid=(batch, nq), name="gqa_attention",
        in_specs=[pl.BlockSpec((tq, W_MIX), lambda b, i: (b * nq + i, 0)),
                  pl.BlockSpec((seq, kvw), lambda b, i: (b, 0)),
                  pl.BlockSpec((2 * kvw, seq), lambda b, i: (0, b))],
        out_specs=pl.BlockSpec((tq, W_MIX), lambda b, i: (b * nq + i, 0)),
        out_shape=jax.ShapeDtypeStruct((batch * seq, W_MIX), BF16),
        scratch_shapes=[pltpu.VMEM((N_KV_GQA * rep * tq // GQA_UNIT_LANES, tk, GQA_UNIT_LANES), F32)] * 2,
        compiler_params=_cparams(("parallel", "parallel")),
    )(qn, kn, v)


def _conv(uc, dw, cb, g, b, wpw_bf, layer, seq):
    n = uc.shape[0]
    ts = _tile_rows(seq, 512)
    nblk = seq // ts
    hb = ts // HALO
    last = n // HALO - 1

    def vec():
        return pl.BlockSpec((None, 1, W_MIX), lambda i: (layer, 0, 0))

    return pl.pallas_call(
        functools.partial(_conv_kernel, ts=ts, nblk=nblk), grid=(n // ts,), name="conformer_conv",
        in_specs=[pl.BlockSpec((ts, 2 * W_MIX), lambda i: (i, 0)),
                  pl.BlockSpec((HALO, 2 * W_MIX), lambda i: (jnp.maximum(i * hb - 1, 0), 0)),
                  pl.BlockSpec((HALO, 2 * W_MIX), lambda i: (jnp.minimum((i + 1) * hb, last), 0)),
                  pl.BlockSpec((None, CONV_WIDTH, W_MIX), lambda i: (layer, 0, 0)),
                  vec(), vec(), vec(),
                  pl.BlockSpec((None, W_MIX, W_MIX), lambda i: (layer, 0, 0))],
        out_specs=pl.BlockSpec((ts, W_MIX), lambda i: (i, 0)),
        out_shape=jax.ShapeDtypeStruct((n, W_MIX), BF16),
        scratch_shapes=[pltpu.VMEM((ts + 2 * HALO, W_MIX), F32), pltpu.VMEM((7, ts + 2 * HALO, W_MIX), F32)],
        compiler_params=_cparams(("parallel",)),
    )(uc, uc, uc, dw, cb, g, b, wpw_bf)


def _bias_tiles(rel_bias, tq):
    idx = jnp.asarray(_bucket_tiles(tq))
    nbr, _, win = idx.shape
    heads = W_MIX // HEAD_DIM
    return pl.pallas_call(
        _bias_kernel, grid=(nbr,), name="dilated_bias_tiles",
        in_specs=[pl.BlockSpec((None, tq, win), lambda r: (r, 0, 0)),
                  pl.BlockSpec(memory_space=pltpu.SMEM)],
        out_specs=pl.BlockSpec((None, heads, tq, win), lambda r: (r, 0, 0, 0)),
        out_shape=jax.ShapeDtypeStruct((nbr, heads, tq, win), F32),
        compiler_params=_cparams(("parallel",)),
    )(idx, rel_bias)


def _dilated(qd, kd, vd, bias, tq, batch, seq):
    heads = W_MIX // HEAD_DIM
    win = tq + 2 * DIL_N
    tiles, nbr, tm = qd.shape[1], qd.shape[2], qd.shape[3]
    max_dil = max(dil for _, dil in DIL_CONFIGS)
    pad_rows = seq + max_dil * 2 * DIL_N
    assert (seq // tq) % 2 == 0
    blk = pl.BlockSpec((None, tiles, None, tm, W_MIX), lambda b, r: (b, 0, r, 0, 0))
    return pl.pallas_call(
        functools.partial(_dilated_kernel, seq=seq, tm=tm, tq=tq), grid=(batch, nbr), name="dilated_attention",
        in_specs=[blk, blk, blk, pl.BlockSpec((None, heads, tq, win), lambda b, r: (r, 0, 0, 0))],
        out_specs=pl.BlockSpec((seq, W_MIX), lambda b, r: (b, 0)),
        out_shape=jax.ShapeDtypeStruct((batch * seq, W_MIX), BF16),
        scratch_shapes=[pltpu.VMEM((seq, W_MIX), BF16), pltpu.VMEM((pad_rows, W_MIX), BF16),
                        pltpu.VMEM((pad_rows, 2 * W_MIX), BF16),
                        pltpu.VMEM((heads, seq, 2 * HEAD_DIM), F32), pltpu.VMEM((heads, seq, 2 * HEAD_DIM), F32),
                        pltpu.VMEM((heads * tq, win), F32), pltpu.VMEM((heads * tq, win), F32)],
        compiler_params=_cparams(("parallel", "arbitrary")),
    )(qd, kd, vd, bias)


def _mix_ffn(x, ya, yb, yc, yd, w_out_bf, g1, b1, w1_bf, w2_bf, g2, b2, layer):
    n = x.shape[0]
    tm = _tile_rows(n, 1024)
    tf = FFN_CHUNK

    def rows(c):
        return pl.BlockSpec((tm, c), lambda i, j: (i, 0))

    def vec():
        return pl.BlockSpec((None, 1, D_MODEL), lambda i, j: (layer, 0, 0))

    return pl.pallas_call(
        functools.partial(_mix_ffn_kernel, nj=D_FF // tf), grid=(n // tm, D_FF // tf), name="outproj_ffn",
        in_specs=[rows(D_MODEL)] + [rows(W_MIX)] * 4
                 + [pl.BlockSpec((None, D_MODEL, D_MODEL), lambda i, j: (layer, 0, 0)), vec(), vec(),
                    pl.BlockSpec((None, D_MODEL, tf), lambda i, j: (layer, 0, j)),
                    pl.BlockSpec((None, tf, D_MODEL), lambda i, j: (layer, j, 0)), vec(), vec()],
        out_specs=rows(D_MODEL), out_shape=jax.ShapeDtypeStruct((n, D_MODEL), F32),
        scratch_shapes=[pltpu.VMEM((tm, D_MODEL), F32), pltpu.VMEM((tm, D_MODEL), F32)],
        compiler_params=_cparams(("parallel", "arbitrary")),
    )(x, ya, yb, yc, yd, w_out_bf, g1, b1, w1_bf, w2_bf, g2, b2)


def kernel(x, emb_ln_g, emb_ln_b, w_in, w_fnet, q_norm_g, k_norm_g, conv_dw, conv_b, conv_ln_g, conv_ln_b,
           w_conv_out, w_out, ln1_g, ln1_b, w_ff1, w_ff2, ln2_g, ln2_b, rel_bias):
    batch, seq, d = x.shape
    depth = w_in.shape[0]
    assert d == D_MODEL and seq % (FFT_R * 16) == 0 and seq % (DIL_CONFIGS[-1][1] * DIL_N) == 0
    n = batch * seq
    a = seq // FFT_R
    heads = W_MIX // HEAD_DIM
    kvw = N_KV_GQA * HEAD_DIM

    cos_np, sin_np = _rope_tables(seq)
    cos, sin = jnp.asarray(cos_np), jnp.asarray(sin_np)
    ones = jnp.asarray(np.kron(np.eye(heads), np.ones((HEAD_DIM, HEAD_DIM))), BF16)
    g1_np, g2_np = _fft_stage_mats(a)
    g1, g2 = jnp.asarray(g1_np, BF16), jnp.asarray(g2_np, BF16)
    twr_np, twi_np = _twiddle(a, seq)
    twr = jnp.broadcast_to(jnp.asarray(twr_np.T)[:, :, None], (FFT_R, a, 128))
    twi = jnp.broadcast_to(jnp.asarray(twi_np.T)[:, :, None], (FFT_R, a, 128))

    w_in_bf = w_in.astype(BF16)
    w_out_bf = w_out.astype(BF16)
    w_ff1_bf = w_ff1.astype(BF16)
    w_ff2_bf = w_ff2.astype(BF16)
    wpw_bf = w_conv_out.astype(BF16)
    gq = jnp.tile(q_norm_g, (1, heads)).reshape(depth, 1, W_MIX)
    gk = jnp.tile(k_norm_g, (1, N_KV_GQA)).reshape(depth, 1, kvw)
    r3 = lambda t: t.reshape(depth, 1, t.shape[-1])

    wfold, wvt_bf = _fold_fourier_weights(w_in, w_fnet, seq)
    tq_d = min(DIL_TQ, seq // MAX_DIL)
    bias = _bias_tiles(rel_bias, tq_d)

    h = _embed_ln(x.reshape(n, d), emb_ln_g, emb_ln_b)
    for l in range(depth):
        zr, zi, qn, kn, v, uc, qd, kd, vd = _inproj(h, wfold, w_in_bf, wvt_bf, cos, sin, gq, gk, ones, l, batch, seq)
        ya = _fourier(zr, zi, g1, g2, twr, twi, batch, seq)
        yb = _gqa(qn, kn, v, batch, seq)
        yd = _dilated(qd, kd, vd, bias, tq_d, batch, seq)
        yc = _conv(uc, conv_dw, r3(conv_b), r3(conv_ln_g), r3(conv_ln_b), wpw_bf, l, seq)
        h = _mix_ffn(h, ya, yb, yc, yd, w_out_bf, r3(ln1_g), r3(ln1_b), w_ff1_bf, w_ff2_bf, r3(ln2_g), r3(ln2_b), l)
    return h.reshape(batch, seq, d)
```

```python
import functools
import math

import numpy as np
import jax
import jax.numpy as jnp
from jax import lax
from jax.experimental import pallas as pl
from jax.experimental.pallas import tpu as pltpu

F32 = jnp.float32
BF16 = jnp.bfloat16

D_MODEL = 1024
DEPTH = 4
HEAD_DIM = 64
W_MIX = 256
N_KV_GQA = 2
CONV_WIDTH = 31
CONV_PAD = CONV_WIDTH // 2
DIL_CONFIGS = ((128, 1), (512, 4), (2048, 16))
DIL_N = 64
MAX_DIL = 16
DIL_TQ = 128
D_FF = 4 * D_MODEL
GRID_W = 64
ROPE_THETA = 10000.0
REL_BUCKETS = 32
REL_MAX_DIST = 1024
ALPHA = (2 * DEPTH) ** 0.25
LN_EPS = 1e-5
RMS_EPS = 1e-6
NEG = -1e30
ATTN_SCALE = HEAD_DIM ** -0.5
LOG2E = math.log2(math.e)

COL_QB, COL_KVB, COL_GLU, COL_QD, COL_KD, COL_VD, D_IN = 256, 512, 768, 1280, 1536, 1792, 2048

FFT_R = 64
INPROJ_ROWS = 1024
INPROJ_SUB_ROWS = 1024
GQA_UNIT_LANES = 256
FFN_CHUNK = 2048
FFN_SUB_ROWS = 256
HALO = 16
VMEM_LIMIT = 56 * 1024 * 1024


def _cparams(sem):
    return pltpu.CompilerParams(dimension_semantics=sem, vmem_limit_bytes=VMEM_LIMIT)


def _dft_cos_sin(n):
    j = np.arange(n)
    ang = 2.0 * np.pi * ((j[:, None] * j[None, :]) % n) / n
    return np.cos(ang), np.sin(ang)


def _channel_dft_blockdiag(seq):
    c, s = _dft_cos_sin(HEAD_DIM)
    scale = 1.0 / math.sqrt(seq * HEAD_DIM)
    eye = np.eye(W_MIX // HEAD_DIM)
    return (np.kron(eye, c) * scale).astype(np.float32), (np.kron(eye, -s) * scale).astype(np.float32)


def _fft_stage_mats(a):
    ca, sa = _dft_cos_sin(a)
    g1 = np.block([[ca, sa], [-sa, ca]])
    c2, s2 = _dft_cos_sin(FFT_R)
    g2 = np.concatenate([c2, s2], axis=1)
    return g1.astype(np.float32), g2.astype(np.float32)


def _twiddle(a, seq):
    c = np.arange(a)[:, None]
    b = np.arange(FFT_R)[None, :]
    ang = 2.0 * np.pi * ((b * c) % seq) / seq
    return np.cos(ang).astype(np.float32), (-np.sin(ang)).astype(np.float32)


def _rope_tables(seq):
    rows = seq // GRID_W
    row = np.repeat(np.arange(rows), GRID_W).astype(np.float32)
    col = np.tile(np.arange(GRID_W), rows).astype(np.float32)
    nf = HEAD_DIM // 4
    inv = (ROPE_THETA ** (-np.arange(nf, dtype=np.float32) / nf)).astype(np.float32)
    ar = row[:, None] * inv
    ac = col[:, None] * inv
    cos = np.concatenate([np.cos(ar), np.cos(ar), np.cos(ac), np.cos(ac)], -1)
    sin = np.concatenate([-np.sin(ar), np.sin(ar), -np.sin(ac), np.sin(ac)], -1)
    reps = W_MIX // HEAD_DIM
    return np.tile(cos, (1, reps)).astype(np.float32), np.tile(sin, (1, reps)).astype(np.float32)


def _t5_bucket_np(rel):
    nb = REL_BUCKETS // 2
    max_exact = nb // 2
    ret = np.where(rel > 0, nb, 0)
    n = np.abs(rel)
    nf = np.maximum(n, 1).astype(np.float32)
    large = max_exact + (np.log(nf / np.float32(max_exact)) / np.float32(math.log(REL_MAX_DIST / max_exact))
                         * np.float32(nb - max_exact)).astype(np.int32)
    large = np.minimum(large, nb - 1)
    return ret + np.where(n < max_exact, n, large)


def _bucket_tiles(tq):
    qi = np.arange(tq)[:, None]
    ki = np.arange(tq + 2 * DIL_N)[None, :]
    rel = ki - DIL_N - qi
    tiles = []
    for _, dil in DIL_CONFIGS:
        tiles.append(np.where(np.abs(rel) <= DIL_N, _t5_bucket_np(rel * dil), -1))
    return np.stack(tiles).astype(np.int32)


def _layer_norm(r, g, b):
    mu = jnp.mean(r, axis=-1, keepdims=True)
    d = r - mu
    var = jnp.mean(d * d, axis=-1, keepdims=True)
    return d * lax.rsqrt(var + LN_EPS) * g + b


def _dot(a, b):
    return jnp.dot(a, b, preferred_element_type=F32)


def _dot_nt(a, b):
    return lax.dot_general(a, b, (((1,), (1,)), ((), ())), preferred_element_type=F32)


def _norm_rope(x, g, cos, sin, ones):
    x2 = x * x
    hi = x2.astype(BF16)
    lo = (x2 - hi.astype(F32)).astype(BF16)
    ss = _dot(hi, ones) + _dot(lo, ones)
    xn = x * lax.rsqrt(ss * (1.0 / HEAD_DIM) + RMS_EPS) * g
    w = x.shape[1]
    lane = lax.broadcasted_iota(jnp.int32, x.shape, 1)
    quarter = HEAD_DIM // 4
    partner = jnp.where((lane % (2 * quarter)) < quarter,
                        pltpu.roll(xn, w - quarter, 1), pltpu.roll(xn, quarter, 1))
    return xn * cos + partner * sin


def _embed_ln_kernel(x_ref, g_ref, b_ref, o_ref):
    o_ref[...] = _layer_norm(x_ref[...], g_ref[...], b_ref[...])


def _fold_kernel(win_ref, wv_ref, wf_ref, c_ref, s_ref, o_ref, wvt_ref):
    wvt_ref[...] = wv_ref[...].T.astype(BF16)
    hp = lax.Precision.HIGHEST
    wf = wf_ref[...]
    mc = jnp.dot(c_ref[...], wf, precision=hp, preferred_element_type=F32)
    ms = jnp.dot(s_ref[...], wf, precision=hp, preferred_element_type=F32)
    win = win_ref[...]
    o_ref[:, :W_MIX] = jnp.dot(win, mc, precision=hp, preferred_element_type=F32).astype(BF16)
    o_ref[:, W_MIX:] = jnp.dot(win, ms, precision=hp, preferred_element_type=F32).astype(BF16)


def _inproj_kernel(x_ref, wfold_ref, w_ref, wvt_ref, cos_ref, sin_ref, gq_ref, gk_ref, ones_ref,
                   zr_ref, zi_ref, qn_ref, kn_ref, vt_ref, uc_ref, qd_ref, kd_ref, vd_ref, stage_q, stage_k, stage_v):
    kvw = N_KV_GQA * HEAD_DIM
    tm = x_ref.shape[0]
    sub = min(INPROJ_SUB_ROWS, tm)
    nsub = tm // sub
    halves = W_MIX // 128

    def project(s):
        xb = x_ref[s * sub:(s + 1) * sub, :].astype(BF16)
        return dict(q=_dot(xb, w_ref[:, COL_QB:COL_KVB]), k=_dot(xb, w_ref[:, COL_KVB:COL_KVB + kvw]),
                    z=_dot(xb, wfold_ref[...]), vt=_dot_nt(wvt_ref[...], xb),
                    dil=[_dot(xb, w_ref[:, col:col + W_MIX]) for col in (COL_QD, COL_KD, COL_VD)],
                    uc=_dot(xb, w_ref[:, COL_GLU:COL_QD]))

    def epilogue(s, p):
        rows = slice(s * sub, (s + 1) * sub)
        uc_ref[rows, :] = p["uc"]
        zr_ref[rows, :] = p["z"][:, :W_MIX]
        zi_ref[rows, :] = p["z"][:, W_MIX:]
        row = lax.broadcasted_iota(jnp.int32, (HEAD_DIM, sub), 0)
        one_row = jnp.where(row == 0, 1.0, 0.0).astype(BF16)
        vt_ref[:, rows] = jnp.concatenate(
            [piece for g in range(N_KV_GQA)
             for piece in (p["vt"][g * HEAD_DIM:(g + 1) * HEAD_DIM].astype(BF16), one_row)], axis=0)

        for u, is_q, out_ref, stage in zip(p["dil"], (True, False, False), (qd_ref, kd_ref, vd_ref),
                                           (stage_q, stage_k, stage_v)):
            if is_q:
                u = u * (ATTN_SCALE * LOG2E)
            for half in range(halves):
                stage[half, rows, :] = u[:, half * 128:(half + 1) * 128]
            for br, (_, dil) in enumerate(DIL_CONFIGS):
                if dil == 1:
                    out_ref[br, rows, :] = u.astype(BF16)
                    continue
                c = tm // dil
                cs = sub // dil
                for r in range(dil):
                    out_ref[br, r * c + s * cs:r * c + (s + 1) * cs, :] = jnp.concatenate(
                        [stage[half, pl.ds(s * sub + r, cs, stride=dil), :] for half in range(halves)],
                        axis=1).astype(BF16)

        cos = cos_ref[rows, :]
        sin = sin_ref[rows, :]
        ones = ones_ref[...]
        qn_ref[rows, :] = (_norm_rope(p["q"], gq_ref[...], cos, sin, ones) * (ATTN_SCALE * LOG2E)).astype(BF16)
        kn_ref[rows, :] = _norm_rope(p["k"], gk_ref[...], cos[:, :kvw], sin[:, :kvw], ones[:kvw, :kvw]).astype(BF16)

    projected = [project(s) for s in range(nsub)]
    for s in range(nsub):
        epilogue(s, projected[s])


def _fourier_kernel(zr_ref, zi_ref, g1_ref, g2_ref, twr_ref, twi_ref, o_ref, yr_scr, yi_scr, *, a):
    g1 = g1_ref[...]
    g2 = g2_ref[...]

    def stage1(b, carry):
        z = jnp.concatenate([zr_ref[pl.ds(b, a, stride=FFT_R), :], zi_ref[pl.ds(b, a, stride=FFT_R), :]], axis=0)
        y = _dot(g1, z.astype(BF16))
        yr, yi = y[:a], y[a:]
        twr = twr_ref[b]
        twi = twi_ref[b]
        row0 = pl.multiple_of(b * a, a)
        yr_scr[pl.ds(row0, a), :] = yr * twr - yi * twi
        yi_scr[pl.ds(row0, a), :] = yr * twi + yi * twr
        return carry

    lax.fori_loop(0, FFT_R, stage1, 0, unroll=8)

    def stage2(c, carry):
        y = jnp.concatenate([yr_scr[pl.ds(c, FFT_R, stride=a), :], yi_scr[pl.ds(c, FFT_R, stride=a), :]], axis=0)
        o_ref[pl.ds(c, FFT_R, stride=a), :] = _dot(g2, y.astype(BF16))
        return carry

    lax.fori_loop(0, a, stage2, 0, unroll=8)


def _gqa_kernel(q_ref, k_ref, vt_ref, o_ref, st_a, st_b, *, tq, tk, seq):
    nk = seq // tk
    rep = (W_MIX // HEAD_DIM) // N_KV_GQA
    zeros = jnp.zeros((tq, HEAD_DIM), BF16)
    q2 = []
    for g in range(N_KV_GQA):
        parts = []
        for r in range(rep):
            h = g * rep + r
            qh = q_ref[:, h * HEAD_DIM:(h + 1) * HEAD_DIM]
            parts.append(jnp.concatenate([zeros] * g + [qh] + [zeros] * (N_KV_GQA - 1 - g), axis=1))
        q2.append(jnp.concatenate(parts, axis=0))
    rows = rep * tq
    vw = 2 * HEAD_DIM

    lanes = GQA_UNIT_LANES
    units = [(g, u) for g in range(N_KV_GQA) for u in range(rows // lanes)]
    q_unit = [q2[g][u * lanes:(u + 1) * lanes, :] for g, u in units]

    def scores(c, st_scr, i):
        off = c * tk if isinstance(c, int) else pl.multiple_of(c * tk, tk)
        st_scr[i] = _dot_nt(k_ref[pl.ds(off, tk), :], q_unit[i])

    def softmax_pv(c, st_scr, i, state):
        off = c * tk if isinstance(c, int) else pl.multiple_of(c * tk, tk)
        g = units[i][0]
        m, acc = state
        st = st_scr[i]
        vt = vt_ref[g * vw:(g + 1) * vw, pl.ds(off, tk)]
        m_new = jnp.maximum(m, jnp.max(st, axis=0, keepdims=True))
        alpha = jnp.exp2(m - m_new)
        pt = jnp.exp2(st - m_new).astype(BF16)
        return m_new, alpha * acc + _dot(vt, pt)

    def half(c_cur, c_next, st_cur, st_next, carry):
        out = []
        for i in range(len(units)):
            if c_next is not None:
                scores(c_next, st_next, i)
            out.append(softmax_pv(c_cur, st_cur, i, carry[i]))
        return tuple(out)

    def body(c2, carry):
        c = 2 * c2
        carry = half(c, c + 1, st_a, st_b, carry)
        return half(c + 1, c + 2, st_b, st_a, carry)

    init = tuple((jnp.full((1, lanes), -jnp.inf, F32), jnp.zeros((vw, lanes), F32)) for _ in units)
    for i in range(len(units)):
        scores(0, st_a, i)
    carry = lax.fori_loop(0, nk // 2 - 1, body, init)
    carry = half(nk - 2, nk - 1, st_a, st_b, carry)
    final = half(nk - 1, None, st_b, None, carry)
    for g in range(N_KV_GQA):
        acc = jnp.concatenate([final[i][1] for i, (gi, _) in enumerate(units) if gi == g], axis=1)
        o = (acc[:HEAD_DIM] / acc[HEAD_DIM:HEAD_DIM + 1]).T
        for r in range(rep):
            h = g * rep + r
            o_ref[:, h * HEAD_DIM:(h + 1) * HEAD_DIM] = o[r * tq:(r + 1) * tq].astype(BF16)


def _conv_kernel(cur_ref, prev_ref, next_ref, dw_ref, cb_ref, g_ref, b_ref, wpw_ref, o_ref, hpad, shifted, *, ts, nblk):
    c = pl.program_id(0) % nblk

    def glu(u):
        return u[:, :W_MIX] * jax.nn.sigmoid(u[:, W_MIX:])

    hpad[0:HALO, :] = jnp.where(c > 0, glu(prev_ref[...]), 0.0)
    hpad[HALO:HALO + ts, :] = glu(cur_ref[...])
    hpad[HALO + ts:HALO + ts + HALO, :] = jnp.where(c < nblk - 1, glu(next_ref[...]), 0.0)
    sublanes = 8
    first = HALO - CONV_PAD
    span = ts + ((first + CONV_WIDTH - 1) // sublanes) * sublanes
    for b in range(1, sublanes):
        shifted[b - 1, 0:span, :] = hpad[b:b + span, :]
    acc = jnp.zeros((ts, W_MIX), F32)
    for j in range(CONV_WIDTH):
        start = first + j
        base = (start // sublanes) * sublanes
        b = start % sublanes
        tap = hpad[base:base + ts, :] if b == 0 else shifted[b - 1, base:base + ts, :]
        acc = acc + dw_ref[j:j + 1, :] * tap
    h = _layer_norm(acc + cb_ref[...], g_ref[...], b_ref[...])
    h = h * jax.nn.sigmoid(h)
    o_ref[...] = _dot(h.astype(BF16), wpw_ref[...]).astype(BF16)


def _bias_kernel(idx_ref, rb_ref, o_ref):
    idx = idx_ref[...]
    for h in range(W_MIX // HEAD_DIM):
        tile = jnp.full(idx.shape, NEG, F32)
        for b in range(REL_BUCKETS):
            tile = jnp.where(idx == b, rb_ref[b, h] * LOG2E, tile)
        o_ref[h] = tile


def _dilated_branch(q_in, k_in, v_in, bias_ref, qbuf, kbuf, vbuf, m_st, acc_st, s_a, s_b,
                    *, dil, first, last, seq, tm, tq):
    n = DIL_N
    heads = W_MIX // HEAD_DIM
    length = seq // dil
    lp = length + 2 * n
    c = tm // dil
    win = tq + 2 * n
    nb = length // tq
    nblocks = dil * nb
    slab = 2 * HEAD_DIM

    lane = lax.broadcasted_iota(jnp.int32, (c, HEAD_DIM), 1)
    one_col = jnp.where(lane == 0, 1.0, 0.0).astype(BF16)
    for r in range(dil):
        base = r * lp
        for buf in (kbuf, vbuf):
            zeros = jnp.zeros((n, buf.shape[1]), BF16)
            buf[base:base + n, :] = zeros
            buf[base + n + length:base + lp, :] = zeros
        for t in range(seq // tm):
            src = slice(r * c, (r + 1) * c)
            qbuf[r * length + t * c:r * length + (t + 1) * c, :] = q_in[t, src, :]
            kbuf[base + n + t * c:base + n + (t + 1) * c, :] = k_in[t, src, :]
            vv = v_in[t, src, :]
            vbuf[base + n + t * c:base + n + (t + 1) * c, :] = jnp.concatenate(
                [piece for h in range(heads) for piece in (vv[:, h * HEAD_DIM:(h + 1) * HEAD_DIM], one_col)], axis=1)

    head_of_lane = lax.broadcasted_iota(jnp.int32, (tq, W_MIX), 1) // HEAD_DIM

    def block_pos(j):
        r = j // nb
        i = j - r * nb
        return r, i, pl.multiple_of(j * tq + r * 2 * n, 2 * n if tq % (2 * n) == 0 else tq)

    def scores(j, s_scr):
        _, _, krow = block_pos(j)
        qb = qbuf[pl.ds(pl.multiple_of(j * tq, tq), tq), :]
        kw = kbuf[pl.ds(krow, win), :]
        qs = jnp.concatenate([jnp.where(head_of_lane == h, qb, jnp.zeros_like(qb)) for h in range(heads)], axis=0)
        s_scr[...] = _dot_nt(qs, kw)

    def update(j, s_scr):
        r, i, krow = block_pos(j)
        vw = vbuf[pl.ds(krow, win), :]
        kpos = i * tq - n + lax.broadcasted_iota(jnp.int32, (tq, win), 1)
        valid = (kpos >= 0) & (kpos < length)
        rows = pl.ds(r + dil * i * tq, tq, stride=dil) if dil > 1 else pl.ds(pl.multiple_of(i * tq, tq), tq)
        for h in range(heads):
            s = jnp.where(valid, s_scr[h * tq:(h + 1) * tq, :] + bias_ref[h], NEG)
            mblk = jnp.max(s, axis=-1, keepdims=True)
            if first:
                m_new = jnp.broadcast_to(mblk, (tq, slab))
            else:
                m_old = m_st[h, rows, :]
                m_new = jnp.maximum(m_old, mblk)
            m_row = jnp.tile(m_new, (1, win // slab)) if win % slab == 0 else m_new[:, :1]
            p = jnp.exp2(s - m_row).astype(BF16)
            pv = _dot(p, vw[:, h * slab:(h + 1) * slab])
            if not first:
                pv = jnp.exp2(m_old - m_new) * acc_st[h, rows, :] + pv
            if not last:
                m_st[h, rows, :] = m_new
            acc_st[h, rows, :] = pv

    def body(j2, carry):
        j = 2 * j2
        scores(j + 1, s_b)
        update(j, s_a)
        scores((j + 2) % nblocks, s_a)
        update(j + 1, s_b)
        return carry

    scores(0, s_a)
    lax.fori_loop(0, nblocks // 2, body, 0)


def _dilated_kernel(q_in, k_in, v_in, bias_ref, o_ref, qbuf, kbuf, vbuf, m_st, acc_st, s_a, s_b, *, seq, tm, tq):
    br = pl.program_id(1)
    nbr = len(DIL_CONFIGS)
    for step in range(nbr):
        dil = DIL_CONFIGS[nbr - 1 - step][1]

        @pl.when(br == step)
        def _(step=step, dil=dil):
            _dilated_branch(q_in, k_in, v_in, bias_ref, qbuf, kbuf, vbuf, m_st, acc_st, s_a, s_b,
                            dil=dil, first=(step == 0), last=(step == nbr - 1), seq=seq, tm=tm, tq=tq)

    @pl.when(br == len(DIL_CONFIGS) - 1)
    def _():
        heads = W_MIX // HEAD_DIM
        rows_per = 256

        def finish(t, carry):
            rows = pl.ds(pl.multiple_of(t * rows_per, rows_per), rows_per)
            outs = []
            for h in range(heads):
                a = acc_st[h, rows, :]
                outs.append(a[:, :HEAD_DIM] / a[:, HEAD_DIM:HEAD_DIM + 1])
            o_ref[rows, :] = jnp.concatenate(outs, axis=1).astype(BF16)
            return carry

        lax.fori_loop(0, seq // rows_per, finish, 0)


def _mix_ffn_kernel(x_ref, ya_ref, yb_ref, yc_ref, yd_ref, wo_ref, g1_ref, b1_ref, w1_ref, w2_ref, g2_ref, b2_ref,
                    o_ref, x1_ref, acc_ref, *, nj):
    j = pl.program_id(1)
    sub = FFN_SUB_ROWS
    nsub = x_ref.shape[0] // sub

    def rows(r):
        return slice(r * sub, (r + 1) * sub)

    def mix(r):
        y_in = jnp.concatenate([ya_ref[rows(r), :].astype(BF16), yb_ref[rows(r), :], yc_ref[rows(r), :],
                                yd_ref[rows(r), :]], axis=1)
        x1 = _layer_norm(ALPHA * x_ref[rows(r), :] + _dot(y_in, wo_ref[...]), g1_ref[...], b1_ref[...])
        x1_ref[rows(r), :] = x1
        return x1.astype(BF16)

    def act(h):
        return jnp.square(jnp.maximum(h, 0.0)).astype(BF16)

    def chunk(first, last):
        def lhs(r):
            return mix(r) if first else x1_ref[rows(r), :].astype(BF16)

        h_next = act(_dot(lhs(0), w1_ref[...]))
        for r in range(nsub):
            h_cur = h_next
            if r + 1 < nsub:
                lhs_next = lhs(r + 1)
            y = _dot(h_cur, w2_ref[...])
            if r + 1 < nsub:
                raw = _dot(lhs_next, w1_ref[...])
            if not first:
                y = y + acc_ref[rows(r), :]
            if last:
                o_ref[rows(r), :] = _layer_norm(ALPHA * x1_ref[rows(r), :] + y, g2_ref[...], b2_ref[...])
            else:
                acc_ref[rows(r), :] = y
            if r + 1 < nsub:
                h_next = act(raw)

    if nj == 1:
        chunk(True, True)
    else:
        pl.when(j == 0)(lambda: chunk(True, False))
        if nj > 2:
            pl.when((j > 0) & (j < nj - 1))(lambda: chunk(False, False))
        pl.when(j == nj - 1)(lambda: chunk(False, True))


def _tile_rows(n, want):
    t = min(n, want)
    assert n % t == 0, (n, t)
    return t


def _embed_ln(x, g, b):
    n, d = x.shape
    tm = _tile_rows(n, 1024)
    return pl.pallas_call(
        _embed_ln_kernel, grid=(n // tm,), name="embed_ln",
        in_specs=[pl.BlockSpec((tm, d), lambda i: (i, 0)), pl.BlockSpec((1, d), lambda i: (0, 0)),
                  pl.BlockSpec((1, d), lambda i: (0, 0))],
        out_specs=pl.BlockSpec((tm, d), lambda i: (i, 0)),
        out_shape=jax.ShapeDtypeStruct((n, d), F32), compiler_params=_cparams(("parallel",)),
    )(x, g.reshape(1, d), b.reshape(1, d))


def _fold_fourier_weights(w_in, w_fnet, seq):
    c_bd, s_bd = _channel_dft_blockdiag(seq)
    depth = w_in.shape[0]
    sq = pl.BlockSpec((W_MIX, W_MIX), lambda l: (0, 0))
    kvw = N_KV_GQA * HEAD_DIM
    v_col = COL_KVB + kvw
    assert v_col % kvw == 0
    return pl.pallas_call(
        _fold_kernel, grid=(depth,), name="fold_weights",
        in_specs=[pl.BlockSpec((None, D_MODEL, W_MIX), lambda l: (l, 0, 0)),
                  pl.BlockSpec((None, D_MODEL, kvw), lambda l: (l, 0, v_col // kvw)),
                  pl.BlockSpec((None, W_MIX, W_MIX), lambda l: (l, 0, 0)), sq, sq],
        out_specs=[pl.BlockSpec((None, D_MODEL, 2 * W_MIX), lambda l: (l, 0, 0)),
                   pl.BlockSpec((None, kvw, D_MODEL), lambda l: (l, 0, 0))],
        out_shape=[jax.ShapeDtypeStruct((depth, D_MODEL, 2 * W_MIX), BF16),
                   jax.ShapeDtypeStruct((depth, kvw, D_MODEL), BF16)],
        compiler_params=_cparams(("parallel",)),
    )(w_in, w_in, w_fnet, jnp.asarray(c_bd), jnp.asarray(s_bd))


def _inproj(x, wfold, w_in_bf, wvt_bf, cos, sin, gq, gk, ones, layer, batch, seq):
    n = x.shape[0]
    tm = _tile_rows(seq, INPROJ_ROWS)
    kvw = N_KV_GQA * HEAD_DIM
    pos_blocks = seq // tm
    nbr = len(DIL_CONFIGS)
    assert DIL_CONFIGS[0][1] == 1 and all(tm % (dil * 16) == 0 for _, dil in DIL_CONFIGS)

    def rows(c):
        return pl.BlockSpec((tm, c), lambda i: (i, 0))

    def whole(r, c):
        return pl.BlockSpec((r, c), lambda i: (0, 0))

    def flat(c, dt):
        return rows(c), jax.ShapeDtypeStruct((n, c), dt)

    dil_out = (pl.BlockSpec((None, None, nbr, tm, W_MIX), lambda i: (i // pos_blocks, i % pos_blocks, 0, 0, 0)),
               jax.ShapeDtypeStruct((batch, pos_blocks, nbr, tm, W_MIX), BF16))
    vt_out = (pl.BlockSpec((2 * kvw, tm), lambda i: (0, i)),
              jax.ShapeDtypeStruct((2 * kvw, n), BF16))
    outs = [flat(W_MIX, F32), flat(W_MIX, F32), flat(W_MIX, BF16), flat(kvw, BF16), vt_out, flat(2 * W_MIX, F32),
            dil_out, dil_out, dil_out]
    return pl.pallas_call(
        _inproj_kernel, grid=(n // tm,), name="inproj",
        in_specs=[rows(D_MODEL),
                  pl.BlockSpec((None, D_MODEL, 2 * W_MIX), lambda i: (layer, 0, 0)),
                  pl.BlockSpec((None, D_MODEL, D_IN), lambda i: (layer, 0, 0)),
                  pl.BlockSpec((None, kvw, D_MODEL), lambda i: (layer, 0, 0)),
                  pl.BlockSpec((tm, W_MIX), lambda i: (i % pos_blocks, 0)),
                  pl.BlockSpec((tm, W_MIX), lambda i: (i % pos_blocks, 0)),
                  pl.BlockSpec((None, 1, W_MIX), lambda i: (layer, 0, 0)),
                  pl.BlockSpec((None, 1, kvw), lambda i: (layer, 0, 0)),
                  whole(W_MIX, W_MIX)],
        out_specs=[o[0] for o in outs], out_shape=[o[1] for o in outs],
        scratch_shapes=[pltpu.VMEM((W_MIX // 128, tm, 128), F32)] * 3,
        compiler_params=_cparams(("parallel",)),
    )(x, wfold, w_in_bf, wvt_bf, cos, sin, gq, gk, ones)


def _fourier(zr, zi, g1, g2, twr, twi, batch, seq):
    a = seq // FFT_R
    blk = pl.BlockSpec((seq, 128), lambda b, j: (b, j))

    def whole(shape):
        return pl.BlockSpec(shape, lambda b, j: (0,) * len(shape))

    return pl.pallas_call(
        functools.partial(_fourier_kernel, a=a), grid=(batch, W_MIX // 128), name="fourier_mix",
        in_specs=[blk, blk, whole((2 * a, 2 * a)), whole((FFT_R, 2 * FFT_R)), whole((FFT_R, a, 128)),
                  whole((FFT_R, a, 128))],
        out_specs=blk, out_shape=jax.ShapeDtypeStruct((batch * seq, W_MIX), F32),
        scratch_shapes=[pltpu.VMEM((seq, 128), F32)] * 2,
        compiler_params=_cparams(("parallel", "parallel")),
    )(zr, zi, g1, g2, twr, twi)


def _gqa(qn, kn, v, batch, seq):
    tq = _tile_rows(seq, 512)
    tk = _tile_rows(seq, 256)
    nq = seq // tq
    kvw = N_KV_GQA * HEAD_DIM
    rep = (W_MIX // HEAD_DIM) // N_KV_GQA
    assert (seq // tk) % 2 == 0
    return pl.pallas_call(
        functools.partial(_gqa_kernel, tq=tq, tk=tk, seq=seq), grid=(batch, nq), name="gqa_attention",
        in_specs=[pl.BlockSpec((tq, W_MIX), lambda b, i: (b * nq + i, 0)),
                  pl.BlockSpec((seq, kvw), lambda b, i: (b, 0)),
                  pl.BlockSpec((2 * kvw, seq), lambda b, i: (0, b))],
        out_specs=pl.BlockSpec((tq, W_MIX), lambda b, i: (b * nq + i, 0)),
        out_shape=jax.ShapeDtypeStruct((batch * seq, W_MIX), BF16),
        scratch_shapes=[pltpu.VMEM((N_KV_GQA * rep * tq // GQA_UNIT_LANES, tk, GQA_UNIT_LANES), F32)] * 2,
        compiler_params=_cparams(("parallel", "parallel")),
    )(qn, kn, v)


def _conv(uc, dw, cb, g, b, wpw_bf, layer, seq):
    n = uc.shape[0]
    ts = _tile_rows(seq, 512)
    nblk = seq // ts
    hb = ts // HALO
    last = n // HALO - 1

    def vec():
        return pl.BlockSpec((None, 1, W_MIX), lambda i: (layer, 0, 0))

    return pl.pallas_call(
        functools.partial(_conv_kernel, ts=ts, nblk=nblk), grid=(n // ts,), name="conformer_conv",
        in_specs=[pl.BlockSpec((ts, 2 * W_MIX), lambda i: (i, 0)),
                  pl.BlockSpec((HALO, 2 * W_MIX), lambda i: (jnp.maximum(i * hb - 1, 0), 0)),
                  pl.BlockSpec((HALO, 2 * W_MIX), lambda i: (jnp.minimum((i + 1) * hb, last), 0)),
                  pl.BlockSpec((None, CONV_WIDTH, W_MIX), lambda i: (layer, 0, 0)),
                  vec(), vec(), vec(),
                  pl.BlockSpec((None, W_MIX, W_MIX), lambda i: (layer, 0, 0))],
        out_specs=pl.BlockSpec((ts, W_MIX), lambda i: (i, 0)),
        out_shape=jax.ShapeDtypeStruct((n, W_MIX), BF16),
        scratch_shapes=[pltpu.VMEM((ts + 2 * HALO, W_MIX), F32), pltpu.VMEM((7, ts + 2 * HALO, W_MIX), F32)],
        compiler_params=_cparams(("parallel",)),
    )(uc, uc, uc, dw, cb, g, b, wpw_bf)


def _bias_tiles(rel_bias, tq):
    idx = jnp.asarray(_bucket_tiles(tq))
    nbr, _, win = idx.shape
    heads = W_MIX // HEAD_DIM
    return pl.pallas_call(
        _bias_kernel, grid=(nbr,), name="dilated_bias_tiles",
        in_specs=[pl.BlockSpec((None, tq, win), lambda r: (r, 0, 0)),
                  pl.BlockSpec(memory_space=pltpu.SMEM)],
        out_specs=pl.BlockSpec((None, heads, tq, win), lambda r: (r, 0, 0, 0)),
        out_shape=jax.ShapeDtypeStruct((nbr, heads, tq, win), F32),
        compiler_params=_cparams(("parallel",)),
    )(idx, rel_bias)


def _dilated(qd, kd, vd, bias, tq, batch, seq):
    heads = W_MIX // HEAD_DIM
    win = tq + 2 * DIL_N
    tiles, nbr, tm = qd.shape[1], qd.shape[2], qd.shape[3]
    max_dil = max(dil for _, dil in DIL_CONFIGS)
    pad_rows = seq + max_dil * 2 * DIL_N
    assert (seq // tq) % 2 == 0
    assert all(a[1] < b[1] for a, b in zip(DIL_CONFIGS, DIL_CONFIGS[1:]))
    blk = pl.BlockSpec((None, tiles, None, tm, W_MIX), lambda b, r: (b, 0, nbr - 1 - r, 0, 0))
    return pl.pallas_call(
        functools.partial(_dilated_kernel, seq=seq, tm=tm, tq=tq), grid=(batch, nbr), name="dilated_attention",
        in_specs=[blk, blk, blk, pl.BlockSpec((None, heads, tq, win), lambda b, r: (nbr - 1 - r, 0, 0, 0))],
        out_specs=pl.BlockSpec((seq, W_MIX), lambda b, r: (b, 0)),
        out_shape=jax.ShapeDtypeStruct((batch * seq, W_MIX), BF16),
        scratch_shapes=[pltpu.VMEM((seq, W_MIX), BF16), pltpu.VMEM((pad_rows, W_MIX), BF16),
                        pltpu.VMEM((pad_rows, 2 * W_MIX), BF16),
                        pltpu.VMEM((heads, seq, 2 * HEAD_DIM), F32), pltpu.VMEM((heads, seq, 2 * HEAD_DIM), F32),
                        pltpu.VMEM((heads * tq, win), F32), pltpu.VMEM((heads * tq, win), F32)],
        compiler_params=_cparams(("parallel", "arbitrary")),
    )(qd, kd, vd, bias)


def _mix_ffn(x, ya, yb, yc, yd, w_out_bf, g1, b1, w1_bf, w2_bf, g2, b2, layer):
    n = x.shape[0]
    tm = _tile_rows(n, 1024)
    tf = FFN_CHUNK

    def rows(c):
        return pl.BlockSpec((tm, c), lambda i, j: (i, 0))

    def vec():
        return pl.BlockSpec((None, 1, D_MODEL), lambda i, j: (layer, 0, 0))

    return pl.pallas_call(
        functools.partial(_mix_ffn_kernel, nj=D_FF // tf), grid=(n // tm, D_FF // tf), name="outproj_ffn",
        in_specs=[rows(D_MODEL)] + [rows(W_MIX)] * 4
                 + [pl.BlockSpec((None, D_MODEL, D_MODEL), lambda i, j: (layer, 0, 0)), vec(), vec(),
                    pl.BlockSpec((None, D_MODEL, tf), lambda i, j: (layer, 0, j)),
                    pl.BlockSpec((None, tf, D_MODEL), lambda i, j: (layer, j, 0)), vec(), vec()],
        out_specs=rows(D_MODEL), out_shape=jax.ShapeDtypeStruct((n, D_MODEL), F32),
        scratch_shapes=[pltpu.VMEM((tm, D_MODEL), F32), pltpu.VMEM((tm, D_MODEL), F32)],
        compiler_params=_cparams(("parallel", "arbitrary")),
    )(x, ya, yb, yc, yd, w_out_bf, g1, b1, w1_bf, w2_bf, g2, b2)


def kernel(x, emb_ln_g, emb_ln_b, w_in, w_fnet, q_norm_g, k_norm_g, conv_dw, conv_b, conv_ln_g, conv_ln_b,
           w_conv_out, w_out, ln1_g, ln1_b, w_ff1, w_ff2, ln2_g, ln2_b, rel_bias):
    batch, seq, d = x.shape
    depth = w_in.shape[0]
    assert d == D_MODEL and seq % (FFT_R * 16) == 0 and seq % (DIL_CONFIGS[-1][1] * DIL_N) == 0
    n = batch * seq
    a = seq // FFT_R
    heads = W_MIX // HEAD_DIM
    kvw = N_KV_GQA * HEAD_DIM

    cos_np, sin_np = _rope_tables(seq)
    cos, sin = jnp.asarray(cos_np), jnp.asarray(sin_np)
    ones = jnp.asarray(np.kron(np.eye(heads), np.ones((HEAD_DIM, HEAD_DIM))), BF16)
    g1_np, g2_np = _fft_stage_mats(a)
    g1, g2 = jnp.asarray(g1_np, BF16), jnp.asarray(g2_np, BF16)
    twr_np, twi_np = _twiddle(a, seq)
    twr = jnp.broadcast_to(jnp.asarray(twr_np.T)[:, :, None], (FFT_R, a, 128))
    twi = jnp.broadcast_to(jnp.asarray(twi_np.T)[:, :, None], (FFT_R, a, 128))

    w_in_bf = w_in.astype(BF16)
    w_out_bf = w_out.astype(BF16)
    w_ff1_bf = w_ff1.astype(BF16)
    w_ff2_bf = w_ff2.astype(BF16)
    wpw_bf = w_conv_out.astype(BF16)
    gq = jnp.tile(q_norm_g, (1, heads)).reshape(depth, 1, W_MIX)
    gk = jnp.tile(k_norm_g, (1, N_KV_GQA)).reshape(depth, 1, kvw)
    r3 = lambda t: t.reshape(depth, 1, t.shape[-1])

    wfold, wvt_bf = _fold_fourier_weights(w_in, w_fnet, seq)
    tq_d = min(DIL_TQ, seq // MAX_DIL)
    bias = _bias_tiles(rel_bias, tq_d)

    h = _embed_ln(x.reshape(n, d), emb_ln_g, emb_ln_b)
    for l in range(depth):
        zr, zi, qn, kn, v, uc, qd, kd, vd = _inproj(h, wfold, w_in_bf, wvt_bf, cos, sin, gq, gk, ones, l, batch, seq)
        ya = _fourier(zr, zi, g1, g2, twr, twi, batch, seq)
        yb = _gqa(qn, kn, v, batch, seq)
        yd = _dilated(qd, kd, vd, bias, tq_d, batch, seq)
        yc = _conv(uc, conv_dw, r3(conv_b), r3(conv_ln_g), r3(conv_ln_b), wpw_bf, l, seq)
        h = _mix_ffn(h, ya, yb, yc, yd, w_out_bf, r3(ln1_g), r3(ln1_b), w_ff1_bf, w_ff2_bf, r3(ln2_g), r3(ln2_b), l)
    return h.reshape(batch, seq, d)
```

```python
import functools
import math

import numpy as np
import jax
import jax.numpy as jnp
from jax import lax
from jax.experimental import pallas as pl
from jax.experimental.pallas import tpu as pltpu

F32 = jnp.float32
BF16 = jnp.bfloat16

D_MODEL = 1024
DEPTH = 4
HEAD_DIM = 64
W_MIX = 256
N_KV_GQA = 2
CONV_WIDTH = 31
CONV_PAD = CONV_WIDTH // 2
DIL_CONFIGS = ((128, 1), (512, 4), (2048, 16))
DIL_N = 64
MAX_DIL = 16
DIL_TQ = 128
D_FF = 4 * D_MODEL
GRID_W = 64
ROPE_THETA = 10000.0
REL_BUCKETS = 32
REL_MAX_DIST = 1024
ALPHA = (2 * DEPTH) ** 0.25
LN_EPS = 1e-5
RMS_EPS = 1e-6
NEG = -1e30
ATTN_SCALE = HEAD_DIM ** -0.5
LOG2E = math.log2(math.e)

COL_QB, COL_KVB, COL_GLU, COL_QD, COL_KD, COL_VD, D_IN = 256, 512, 768, 1280, 1536, 1792, 2048

FFT_R = 64
INPROJ_ROWS = 1024
INPROJ_SUB_ROWS = 1024
GQA_UNIT_LANES = 256
FFN_CHUNK = 2048
FFN_SUB_ROWS = 256
HALO = 16
VMEM_LIMIT = 56 * 1024 * 1024


def _cparams(sem):
    return pltpu.CompilerParams(dimension_semantics=sem, vmem_limit_bytes=VMEM_LIMIT)


def _dft_cos_sin(n):
    j = np.arange(n)
    ang = 2.0 * np.pi * ((j[:, None] * j[None, :]) % n) / n
    return np.cos(ang), np.sin(ang)


def _channel_dft_blockdiag(seq):
    c, s = _dft_cos_sin(HEAD_DIM)
    scale = 1.0 / math.sqrt(seq * HEAD_DIM)
    eye = np.eye(W_MIX // HEAD_DIM)
    return (np.kron(eye, c) * scale).astype(np.float32), (np.kron(eye, -s) * scale).astype(np.float32)


def _fft_stage_mats(a):
    ca, sa = _dft_cos_sin(a)
    g1 = np.block([[ca, sa], [-sa, ca]])
    c2, s2 = _dft_cos_sin(FFT_R)
    g2 = np.concatenate([c2, s2], axis=1)
    return g1.astype(np.float32), g2.astype(np.float32)


def _twiddle(a, seq):
    c = np.arange(a)[:, None]
    b = np.arange(FFT_R)[None, :]
    ang = 2.0 * np.pi * ((b * c) % seq) / seq
    return np.cos(ang).astype(np.float32), (-np.sin(ang)).astype(np.float32)


def _rope_tables(seq):
    rows = seq // GRID_W
    row = np.repeat(np.arange(rows), GRID_W).astype(np.float32)
    col = np.tile(np.arange(GRID_W), rows).astype(np.float32)
    nf = HEAD_DIM // 4
    inv = (ROPE_THETA ** (-np.arange(nf, dtype=np.float32) / nf)).astype(np.float32)
    ar = row[:, None] * inv
    ac = col[:, None] * inv
    cos = np.concatenate([np.cos(ar), np.cos(ar), np.cos(ac), np.cos(ac)], -1)
    sin = np.concatenate([-np.sin(ar), np.sin(ar), -np.sin(ac), np.sin(ac)], -1)
    reps = W_MIX // HEAD_DIM
    return np.tile(cos, (1, reps)).astype(np.float32), np.tile(sin, (1, reps)).astype(np.float32)


def _t5_bucket_np(rel):
    nb = REL_BUCKETS // 2
    max_exact = nb // 2
    ret = np.where(rel > 0, nb, 0)
    n = np.abs(rel)
    nf = np.maximum(n, 1).astype(np.float32)
    large = max_exact + (np.log(nf / np.float32(max_exact)) / np.float32(math.log(REL_MAX_DIST / max_exact))
                         * np.float32(nb - max_exact)).astype(np.int32)
    large = np.minimum(large, nb - 1)
    return ret + np.where(n < max_exact, n, large)


def _bucket_tiles(tq):
    qi = np.arange(tq)[:, None]
    ki = np.arange(tq + 2 * DIL_N)[None, :]
    rel = ki - DIL_N - qi
    tiles = []
    for _, dil in DIL_CONFIGS:
        tiles.append(np.where(np.abs(rel) <= DIL_N, _t5_bucket_np(rel * dil), -1))
    return np.stack(tiles).astype(np.int32)


def _padded_pitch(n):
    return n + 8


def _layer_norm(r, g, b):
    mu = jnp.mean(r, axis=-1, keepdims=True)
    d = r - mu
    var = jnp.mean(d * d, axis=-1, keepdims=True)
    return d * lax.rsqrt(var + LN_EPS) * g + b


def _dot(a, b):
    return jnp.dot(a, b, preferred_element_type=F32)


def _dot_nt(a, b):
    return lax.dot_general(a, b, (((1,), (1,)), ((), ())), preferred_element_type=F32)


def _norm_rope(x, g, cos, sin, ones):
    x2 = x * x
    hi = x2.astype(BF16)
    lo = (x2 - hi.astype(F32)).astype(BF16)
    ss = _dot(hi, ones) + _dot(lo, ones)
    xn = x * lax.rsqrt(ss * (1.0 / HEAD_DIM) + RMS_EPS) * g
    w = x.shape[1]
    lane = lax.broadcasted_iota(jnp.int32, x.shape, 1)
    quarter = HEAD_DIM // 4
    partner = jnp.where((lane % (2 * quarter)) < quarter,
                        pltpu.roll(xn, w - quarter, 1), pltpu.roll(xn, quarter, 1))
    return xn * cos + partner * sin


def _embed_ln_kernel(x_ref, g_ref, b_ref, o_ref):
    o_ref[...] = _layer_norm(x_ref[...], g_ref[...], b_ref[...])


def _fold_kernel(win_ref, wv_ref, wf_ref, c_ref, s_ref, o_ref, wvt_ref):
    wvt_ref[...] = wv_ref[...].T.astype(BF16)
    hp = lax.Precision.HIGHEST
    wf = wf_ref[...]
    mc = jnp.dot(c_ref[...], wf, precision=hp, preferred_element_type=F32)
    ms = jnp.dot(s_ref[...], wf, precision=hp, preferred_element_type=F32)
    win = win_ref[...]
    o_ref[:, :W_MIX] = jnp.dot(win, mc, precision=hp, preferred_element_type=F32).astype(BF16)
    o_ref[:, W_MIX:] = jnp.dot(win, ms, precision=hp, preferred_element_type=F32).astype(BF16)


def _inproj_kernel(x_ref, wfold_ref, w_ref, wvt_ref, cos_ref, sin_ref, gq_ref, gk_ref, ones_ref,
                   zr_ref, zi_ref, qn_ref, kn_ref, vt_ref, uc_ref, qd_ref, kd_ref, vd_ref, stage_q, stage_k, stage_v,
                   pad_q, pad_k, pad_v):
    kvw = N_KV_GQA * HEAD_DIM
    tm = x_ref.shape[0]
    sub = min(INPROJ_SUB_ROWS, tm)
    nsub = tm // sub
    halves = W_MIX // 128

    def project(s):
        xb = x_ref[s * sub:(s + 1) * sub, :].astype(BF16)
        return dict(q=_dot(xb, w_ref[:, COL_QB:COL_KVB]), k=_dot(xb, w_ref[:, COL_KVB:COL_KVB + kvw]),
                    z=_dot(xb, wfold_ref[...]), vt=_dot_nt(wvt_ref[...], xb),
                    dil=[_dot(xb, w_ref[:, col:col + W_MIX]) for col in (COL_QD, COL_KD, COL_VD)],
                    uc=_dot(xb, w_ref[:, COL_GLU:COL_QD]))

    def epilogue(s, p):
        rows = slice(s * sub, (s + 1) * sub)
        uc_ref[rows, :] = p["uc"]
        pin = _padded_pitch(FFT_R)
        pad = jnp.zeros((pin - FFT_R, W_MIX), F32)
        for g in range(sub // FFT_R):
            dst = (s * sub // FFT_R + g) * pin
            src = slice(g * FFT_R, (g + 1) * FFT_R)
            for z_ref, lanes in ((zr_ref, slice(0, W_MIX)), (zi_ref, slice(W_MIX, 2 * W_MIX))):
                z_ref[dst:dst + FFT_R, :] = p["z"][src, lanes]
                z_ref[dst + FFT_R:dst + pin, :] = pad
        row = lax.broadcasted_iota(jnp.int32, (HEAD_DIM, sub), 0)
        one_row = jnp.where(row == 0, 1.0, 0.0).astype(BF16)
        vt_ref[:, rows] = jnp.concatenate(
            [piece for g in range(N_KV_GQA)
             for piece in (p["vt"][g * HEAD_DIM:(g + 1) * HEAD_DIM].astype(BF16), one_row)], axis=0)

        for u, is_q, out_ref, stage, stage_pad in zip(p["dil"], (True, False, False), (qd_ref, kd_ref, vd_ref),
                                                      (stage_q, stage_k, stage_v), (pad_q, pad_k, pad_v)):
            if is_q:
                u = u * (ATTN_SCALE * LOG2E)
            for half in range(halves):
                stage[half, rows, :] = u[:, half * 128:(half + 1) * 128]
            for br, (_, dil) in enumerate(DIL_CONFIGS):
                if dil == 1:
                    out_ref[br, rows, :] = u.astype(BF16)
                    continue
                c = tm // dil
                cs = sub // dil
                if dil % 8 == 0:
                    pitch = _padded_pitch(dil)
                    for half in range(halves):
                        for j in range(cs):
                            stage_pad[half, j * pitch:j * pitch + dil, :] = u[j * dil:(j + 1) * dil,
                                                                              half * 128:(half + 1) * 128]
                    src, start, stride = stage_pad, 0, pitch
                else:
                    src, start, stride = stage, s * sub, dil
                for r in range(dil):
                    out_ref[br, r * c + s * cs:r * c + (s + 1) * cs, :] = jnp.concatenate(
                        [src[half, pl.ds(start + r, cs, stride=stride), :] for half in range(halves)],
                        axis=1).astype(BF16)

        cos = cos_ref[rows, :]
        sin = sin_ref[rows, :]
        ones = ones_ref[...]
        qn_ref[rows, :] = (_norm_rope(p["q"], gq_ref[...], cos, sin, ones) * (ATTN_SCALE * LOG2E)).astype(BF16)
        kn_ref[rows, :] = _norm_rope(p["k"], gk_ref[...], cos[:, :kvw], sin[:, :kvw], ones[:kvw, :kvw]).astype(BF16)

    projected = [project(s) for s in range(nsub)]
    for s in range(nsub):
        epilogue(s, projected[s])


def _fourier_kernel(zr_ref, zi_ref, g1_ref, g2_ref, twr_ref, twi_ref, o_ref, yr_scr, yi_scr, *, a):
    pin = _padded_pitch(FFT_R)
    pmid = _padded_pitch(a)
    g1 = g1_ref[...]
    g2 = g2_ref[...]

    def stage1(b, carry):
        z = jnp.concatenate([zr_ref[pl.ds(b, a, stride=pin), :], zi_ref[pl.ds(b, a, stride=pin), :]], axis=0)
        y = _dot(g1, z.astype(BF16))
        yr, yi = y[:a], y[a:]
        twr = twr_ref[b]
        twi = twi_ref[b]
        row0 = pl.multiple_of(b * pmid, 8)
        yr_scr[pl.ds(row0, a), :] = yr * twr - yi * twi
        yi_scr[pl.ds(row0, a), :] = yr * twi + yi * twr
        return carry

    lax.fori_loop(0, FFT_R, stage1, 0, unroll=8)

    pad = jnp.zeros((pmid - a, 128), F32)
    for e in range(FFT_R):
        o_ref[e * pmid + a:(e + 1) * pmid, :] = pad

    def stage2(c, carry):
        y = jnp.concatenate([yr_scr[pl.ds(c, FFT_R, stride=pmid), :], yi_scr[pl.ds(c, FFT_R, stride=pmid), :]], axis=0)
        o_ref[pl.ds(c, FFT_R, stride=pmid), :] = _dot(g2, y.astype(BF16))
        return carry

    lax.fori_loop(0, a, stage2, 0, unroll=8)


def _gqa_kernel(q_ref, k_ref, vt_ref, o_ref, st_a, st_b, *, tq, tk, seq):
    nk = seq // tk
    rep = (W_MIX // HEAD_DIM) // N_KV_GQA
    zeros = jnp.zeros((tq, HEAD_DIM), BF16)
    q2 = []
    for g in range(N_KV_GQA):
        parts = []
        for r in range(rep):
            h = g * rep + r
            qh = q_ref[:, h * HEAD_DIM:(h + 1) * HEAD_DIM]
            parts.append(jnp.concatenate([zeros] * g + [qh] + [zeros] * (N_KV_GQA - 1 - g), axis=1))
        q2.append(jnp.concatenate(parts, axis=0))
    rows = rep * tq
    vw = 2 * HEAD_DIM

    lanes = GQA_UNIT_LANES
    units = [(g, u) for g in range(N_KV_GQA) for u in range(rows // lanes)]
    q_unit = [q2[g][u * lanes:(u + 1) * lanes, :] for g, u in units]

    def scores(c, st_scr, i):
        off = c * tk if isinstance(c, int) else pl.multiple_of(c * tk, tk)
        st_scr[i] = _dot_nt(k_ref[pl.ds(off, tk), :], q_unit[i])

    def softmax_pv(c, st_scr, i, state):
        off = c * tk if isinstance(c, int) else pl.multiple_of(c * tk, tk)
        g = units[i][0]
        m, acc = state
        st = st_scr[i]
        vt = vt_ref[g * vw:(g + 1) * vw, pl.ds(off, tk)]
        m_new = jnp.maximum(m, jnp.max(st, axis=0, keepdims=True))
        alpha = jnp.exp2(m - m_new)
        pt = jnp.exp2(st - m_new).astype(BF16)
        return m_new, alpha * acc + _dot(vt, pt)

    def half(c_cur, c_next, st_cur, st_next, carry):
        out = []
        for i in range(len(units)):
            if c_next is not None:
                scores(c_next, st_next, i)
            out.append(softmax_pv(c_cur, st_cur, i, carry[i]))
        return tuple(out)

    def body(c2, carry):
        c = 2 * c2
        carry = half(c, c + 1, st_a, st_b, carry)
        return half(c + 1, c + 2, st_b, st_a, carry)

    init = tuple((jnp.full((1, lanes), -jnp.inf, F32), jnp.zeros((vw, lanes), F32)) for _ in units)
    for i in range(len(units)):
        scores(0, st_a, i)
    carry = lax.fori_loop(0, nk // 2 - 1, body, init)
    carry = half(nk - 2, nk - 1, st_a, st_b, carry)
    final = half(nk - 1, None, st_b, None, carry)
    for g in range(N_KV_GQA):
        acc = jnp.concatenate([final[i][1] for i, (gi, _) in enumerate(units) if gi == g], axis=1)
        o = (acc[:HEAD_DIM] / acc[HEAD_DIM:HEAD_DIM + 1]).T
        for r in range(rep):
            h = g * rep + r
            o_ref[:, h * HEAD_DIM:(h + 1) * HEAD_DIM] = o[r * tq:(r + 1) * tq].astype(BF16)


def _conv_kernel(cur_ref, prev_ref, next_ref, dw_ref, cb_ref, g_ref, b_ref, wpw_ref, o_ref, hpad, shifted, *, ts, nblk):
    c = pl.program_id(0) % nblk

    def glu(u):
        return u[:, :W_MIX] * jax.nn.sigmoid(u[:, W_MIX:])

    hpad[0:HALO, :] = jnp.where(c > 0, glu(prev_ref[...]), 0.0)
    hpad[HALO:HALO + ts, :] = glu(cur_ref[...])
    hpad[HALO + ts:HALO + ts + HALO, :] = jnp.where(c < nblk - 1, glu(next_ref[...]), 0.0)
    sublanes = 8
    first = HALO - CONV_PAD
    span = ts + ((first + CONV_WIDTH - 1) // sublanes) * sublanes
    for b in range(1, sublanes):
        shifted[b - 1, 0:span, :] = hpad[b:b + span, :]
    acc = jnp.zeros((ts, W_MIX), F32)
    for j in range(CONV_WIDTH):
        start = first + j
        base = (start // sublanes) * sublanes
        b = start % sublanes
        tap = hpad[base:base + ts, :] if b == 0 else shifted[b - 1, base:base + ts, :]
        acc = acc + dw_ref[j:j + 1, :] * tap
    h = _layer_norm(acc + cb_ref[...], g_ref[...], b_ref[...])
    h = h * jax.nn.sigmoid(h)
    o_ref[...] = _dot(h.astype(BF16), wpw_ref[...]).astype(BF16)


def _bias_kernel(idx_ref, rb_ref, o_ref):
    idx = idx_ref[...]
    for h in range(W_MIX // HEAD_DIM):
        tile = jnp.full(idx.shape, NEG, F32)
        for b in range(REL_BUCKETS):
            tile = jnp.where(idx == b, rb_ref[b, h] * LOG2E, tile)
        o_ref[h] = tile


def _dilated_branch(q_in, k_in, v_in, bias_ref, qbuf, kbuf, vbuf, m_st, acc_st, s_a, s_b,
                    *, dil, first, last, seq, tm, tq):
    n = DIL_N
    heads = W_MIX // HEAD_DIM
    length = seq // dil
    lp = length + 2 * n
    c = tm // dil
    win = tq + 2 * n
    nb = length // tq
    nblocks = dil * nb
    slab = 2 * HEAD_DIM

    lane = lax.broadcasted_iota(jnp.int32, (c, HEAD_DIM), 1)
    one_col = jnp.where(lane == 0, 1.0, 0.0).astype(BF16)
    for r in range(dil):
        base = r * lp
        for buf in (kbuf, vbuf):
            zeros = jnp.zeros((n, buf.shape[1]), BF16)
            buf[base:base + n, :] = zeros
            buf[base + n + length:base + lp, :] = zeros
        for t in range(seq // tm):
            src = slice(r * c, (r + 1) * c)
            qbuf[r * length + t * c:r * length + (t + 1) * c, :] = q_in[t, src, :]
            kbuf[base + n + t * c:base + n + (t + 1) * c, :] = k_in[t, src, :]
            vv = v_in[t, src, :]
            vbuf[base + n + t * c:base + n + (t + 1) * c, :] = jnp.concatenate(
                [piece for h in range(heads) for piece in (vv[:, h * HEAD_DIM:(h + 1) * HEAD_DIM], one_col)], axis=1)

    head_of_lane = lax.broadcasted_iota(jnp.int32, (tq, W_MIX), 1) // HEAD_DIM

    def block_pos(j):
        r = j // nb
        i = j - r * nb
        return r, i, pl.multiple_of(j * tq + r * 2 * n, 2 * n if tq % (2 * n) == 0 else tq)

    def scores(j, s_scr):
        _, _, krow = block_pos(j)
        qb = qbuf[pl.ds(pl.multiple_of(j * tq, tq), tq), :]
        kw = kbuf[pl.ds(krow, win), :]
        qs = jnp.concatenate([jnp.where(head_of_lane == h, qb, jnp.zeros_like(qb)) for h in range(heads)], axis=0)
        s_scr[...] = _dot_nt(qs, kw)

    def update(j, s_scr):
        r, i, krow = block_pos(j)
        vw = vbuf[pl.ds(krow, win), :]
        kpos = i * tq - n + lax.broadcasted_iota(jnp.int32, (tq, win), 1)
        valid = (kpos >= 0) & (kpos < length)
        rows = pl.ds(r + dil * i * tq, tq, stride=dil) if dil > 1 else pl.ds(pl.multiple_of(i * tq, tq), tq)
        for h in range(heads):
            s = jnp.where(valid, s_scr[h * tq:(h + 1) * tq, :] + bias_ref[h], NEG)
            mblk = jnp.max(s, axis=-1, keepdims=True)
            if first:
                m_new = jnp.broadcast_to(mblk, (tq, slab))
            else:
                m_old = m_st[h, rows, :]
                m_new = jnp.maximum(m_old, mblk)
            m_row = jnp.tile(m_new, (1, win // slab)) if win % slab == 0 else m_new[:, :1]
            p = jnp.exp2(s - m_row).astype(BF16)
            pv = _dot(p, vw[:, h * slab:(h + 1) * slab])
            if not first:
                pv = jnp.exp2(m_old - m_new) * acc_st[h, rows, :] + pv
            if not last:
                m_st[h, rows, :] = m_new
            acc_st[h, rows, :] = pv

    def body(j2, carry):
        j = 2 * j2
        scores(j + 1, s_b)
        update(j, s_a)
        scores((j + 2) % nblocks, s_a)
        update(j + 1, s_b)
        return carry

    scores(0, s_a)
    lax.fori_loop(0, nblocks // 2, body, 0)


def _dilated_kernel(q_in, k_in, v_in, bias_ref, o_ref, qbuf, kbuf, vbuf, m_st, acc_st, s_a, s_b, *, seq, tm, tq):
    br = pl.program_id(1)
    nbr = len(DIL_CONFIGS)
    for step in range(nbr):
        dil = DIL_CONFIGS[nbr - 1 - step][1]

        @pl.when(br == step)
        def _(step=step, dil=dil):
            _dilated_branch(q_in, k_in, v_in, bias_ref, qbuf, kbuf, vbuf, m_st, acc_st, s_a, s_b,
                            dil=dil, first=(step == 0), last=(step == nbr - 1), seq=seq, tm=tm, tq=tq)

    @pl.when(br == len(DIL_CONFIGS) - 1)
    def _():
        heads = W_MIX // HEAD_DIM
        rows_per = 256

        def finish(t, carry):
            rows = pl.ds(pl.multiple_of(t * rows_per, rows_per), rows_per)
            outs = []
            for h in range(heads):
                a = acc_st[h, rows, :]
                outs.append(a[:, :HEAD_DIM] / a[:, HEAD_DIM:HEAD_DIM + 1])
            o_ref[rows, :] = jnp.concatenate(outs, axis=1).astype(BF16)
            return carry

        lax.fori_loop(0, seq // rows_per, finish, 0)


def _mix_ffn_kernel(x_ref, ya_ref, yb_ref, yc_ref, yd_ref, wo_ref, g1_ref, b1_ref, w1_ref, w2_ref, g2_ref, b2_ref,
                    o_ref, x1_ref, acc_ref, *, nj, a):
    j = pl.program_id(1)
    sub = FFN_SUB_ROWS
    nsub = x_ref.shape[0] // sub

    def rows(r):
        return slice(r * sub, (r + 1) * sub)

    def mix(r):
        pitch = _padded_pitch(a)
        ya = jnp.concatenate([ya_ref[g * pitch:g * pitch + a, :] for g in range(r * sub // a, (r + 1) * sub // a)],
                             axis=0)
        y_in = jnp.concatenate([ya.astype(BF16), yb_ref[rows(r), :], yc_ref[rows(r), :], yd_ref[rows(r), :]], axis=1)
        x1 = _layer_norm(ALPHA * x_ref[rows(r), :] + _dot(y_in, wo_ref[...]), g1_ref[...], b1_ref[...])
        x1_ref[rows(r), :] = x1
        return x1.astype(BF16)

    def act(h):
        return jnp.square(jnp.maximum(h, 0.0)).astype(BF16)

    def chunk(first, last):
        def lhs(r):
            return mix(r) if first else x1_ref[rows(r), :].astype(BF16)

        h_next = act(_dot(lhs(0), w1_ref[...]))
        for r in range(nsub):
            h_cur = h_next
            if r + 1 < nsub:
                lhs_next = lhs(r + 1)
            y = _dot(h_cur, w2_ref[...])
            if r + 1 < nsub:
                raw = _dot(lhs_next, w1_ref[...])
            if not first:
                y = y + acc_ref[rows(r), :]
            if last:
                o_ref[rows(r), :] = _layer_norm(ALPHA * x1_ref[rows(r), :] + y, g2_ref[...], b2_ref[...])
            else:
                acc_ref[rows(r), :] = y
            if r + 1 < nsub:
                h_next = act(raw)

    if nj == 1:
        chunk(True, True)
    else:
        pl.when(j == 0)(lambda: chunk(True, False))
        if nj > 2:
            pl.when((j > 0) & (j < nj - 1))(lambda: chunk(False, False))
        pl.when(j == nj - 1)(lambda: chunk(False, True))


def _tile_rows(n, want):
    t = min(n, want)
    assert n % t == 0, (n, t)
    return t


def _embed_ln(x, g, b):
    n, d = x.shape
    tm = _tile_rows(n, 1024)
    return pl.pallas_call(
        _embed_ln_kernel, grid=(n // tm,), name="embed_ln",
        in_specs=[pl.BlockSpec((tm, d), lambda i: (i, 0)), pl.BlockSpec((1, d), lambda i: (0, 0)),
                  pl.BlockSpec((1, d), lambda i: (0, 0))],
        out_specs=pl.BlockSpec((tm, d), lambda i: (i, 0)),
        out_shape=jax.ShapeDtypeStruct((n, d), F32), compiler_params=_cparams(("parallel",)),
    )(x, g.reshape(1, d), b.reshape(1, d))


def _fold_fourier_weights(w_in, w_fnet, seq):
    c_bd, s_bd = _channel_dft_blockdiag(seq)
    depth = w_in.shape[0]
    sq = pl.BlockSpec((W_MIX, W_MIX), lambda l: (0, 0))
    kvw = N_KV_GQA * HEAD_DIM
    v_col = COL_KVB + kvw
    assert v_col % kvw == 0
    return pl.pallas_call(
        _fold_kernel, grid=(depth,), name="fold_weights",
        in_specs=[pl.BlockSpec((None, D_MODEL, W_MIX), lambda l: (l, 0, 0)),
                  pl.BlockSpec((None, D_MODEL, kvw), lambda l: (l, 0, v_col // kvw)),
                  pl.BlockSpec((None, W_MIX, W_MIX), lambda l: (l, 0, 0)), sq, sq],
        out_specs=[pl.BlockSpec((None, D_MODEL, 2 * W_MIX), lambda l: (l, 0, 0)),
                   pl.BlockSpec((None, kvw, D_MODEL), lambda l: (l, 0, 0))],
        out_shape=[jax.ShapeDtypeStruct((depth, D_MODEL, 2 * W_MIX), BF16),
                   jax.ShapeDtypeStruct((depth, kvw, D_MODEL), BF16)],
        compiler_params=_cparams(("parallel",)),
    )(w_in, w_in, w_fnet, jnp.asarray(c_bd), jnp.asarray(s_bd))


def _inproj(x, wfold, w_in_bf, wvt_bf, cos, sin, gq, gk, ones, layer, batch, seq):
    n = x.shape[0]
    tm = _tile_rows(seq, INPROJ_ROWS)
    kvw = N_KV_GQA * HEAD_DIM
    pos_blocks = seq // tm
    nbr = len(DIL_CONFIGS)
    assert DIL_CONFIGS[0][1] == 1 and all(tm % (dil * 16) == 0 for _, dil in DIL_CONFIGS)

    def rows(c):
        return pl.BlockSpec((tm, c), lambda i: (i, 0))

    def whole(r, c):
        return pl.BlockSpec((r, c), lambda i: (0, 0))

    def flat(c, dt):
        return rows(c), jax.ShapeDtypeStruct((n, c), dt)

    dil_out = (pl.BlockSpec((None, None, nbr, tm, W_MIX), lambda i: (i // pos_blocks, i % pos_blocks, 0, 0, 0)),
               jax.ShapeDtypeStruct((batch, pos_blocks, nbr, tm, W_MIX), BF16))
    vt_out = (pl.BlockSpec((2 * kvw, tm), lambda i: (0, i)),
              jax.ShapeDtypeStruct((2 * kvw, n), BF16))
    z_rows = tm // FFT_R * _padded_pitch(FFT_R)
    z_out = (pl.BlockSpec((z_rows, W_MIX), lambda i: (i, 0)), jax.ShapeDtypeStruct((n // tm * z_rows, W_MIX), F32))
    outs = [z_out, z_out, flat(W_MIX, BF16), flat(kvw, BF16), vt_out, flat(2 * W_MIX, F32),
            dil_out, dil_out, dil_out]
    return pl.pallas_call(
        _inproj_kernel, grid=(n // tm,), name="inproj",
        in_specs=[rows(D_MODEL),
                  pl.BlockSpec((None, D_MODEL, 2 * W_MIX), lambda i: (layer, 0, 0)),
                  pl.BlockSpec((None, D_MODEL, D_IN), lambda i: (layer, 0, 0)),
                  pl.BlockSpec((None, kvw, D_MODEL), lambda i: (layer, 0, 0)),
                  pl.BlockSpec((tm, W_MIX), lambda i: (i % pos_blocks, 0)),
                  pl.BlockSpec((tm, W_MIX), lambda i: (i % pos_blocks, 0)),
                  pl.BlockSpec((None, 1, W_MIX), lambda i: (layer, 0, 0)),
                  pl.BlockSpec((None, 1, kvw), lambda i: (layer, 0, 0)),
                  whole(W_MIX, W_MIX)],
        out_specs=[o[0] for o in outs], out_shape=[o[1] for o in outs],
        scratch_shapes=[pltpu.VMEM((W_MIX // 128, tm, 128), F32)] * 3
                       + [pltpu.VMEM((W_MIX // 128, tm // MAX_DIL * _padded_pitch(MAX_DIL), 128), F32)] * 3,
        compiler_params=_cparams(("parallel",)),
    )(x, wfold, w_in_bf, wvt_bf, cos, sin, gq, gk, ones)


def _fourier(zr, zi, g1, g2, twr, twi, batch, seq):
    a = seq // FFT_R
    rows_in = a * _padded_pitch(FFT_R)
    rows_mid = FFT_R * _padded_pitch(a)
    blk_in = pl.BlockSpec((rows_in, 128), lambda b, j: (b, j))

    def whole(shape):
        return pl.BlockSpec(shape, lambda b, j: (0,) * len(shape))

    return pl.pallas_call(
        functools.partial(_fourier_kernel, a=a), grid=(batch, W_MIX // 128), name="fourier_mix",
        in_specs=[blk_in, blk_in, whole((2 * a, 2 * a)), whole((FFT_R, 2 * FFT_R)), whole((FFT_R, a, 128)),
                  whole((FFT_R, a, 128))],
        out_specs=pl.BlockSpec((rows_mid, 128), lambda b, j: (b, j)),
        out_shape=jax.ShapeDtypeStruct((batch * rows_mid, W_MIX), F32),
        scratch_shapes=[pltpu.VMEM((rows_mid, 128), F32)] * 2,
        compiler_params=_cparams(("parallel", "parallel")),
    )(zr, zi, g1, g2, twr, twi)


def _gqa(qn, kn, v, batch, seq):
    tq = _tile_rows(seq, 512)
    tk = _tile_rows(seq, 256)
    nq = seq // tq
    kvw = N_KV_GQA * HEAD_DIM
    rep = (W_MIX // HEAD_DIM) // N_KV_GQA
    assert (seq // tk) % 2 == 0
    return pl.pallas_call(
        functools.partial(_gqa_kernel, tq=tq, tk=tk, seq=seq), grid=(batch, nq), name="gqa_attention",
        in_specs=[pl.BlockSpec((tq, W_MIX), lambda b, i: (b * nq + i, 0)),
                  pl.BlockSpec((seq, kvw), lambda b, i: (b, 0)),
                  pl.BlockSpec((2 * kvw, seq), lambda b, i: (0, b))],
        out_specs=pl.BlockSpec((tq, W_MIX), lambda b, i: (b * nq + i, 0)),
        out_shape=jax.ShapeDtypeStruct((batch * seq, W_MIX), BF16),
        scratch_shapes=[pltpu.VMEM((N_KV_GQA * rep * tq // GQA_UNIT_LANES, tk, GQA_UNIT_LANES), F32)] * 2,
        compiler_params=_cparams(("parallel", "parallel")),
    )(qn, kn, v)


def _conv(uc, dw, cb, g, b, wpw_bf, layer, seq):
    n = uc.shape[0]
    ts = _tile_rows(seq, 512)
    nblk = seq // ts
    hb = ts // HALO
    last = n // HALO - 1

    def vec():
        return pl.BlockSpec((None, 1, W_MIX), lambda i: (layer, 0, 0))

    return pl.pallas_call(
        functools.partial(_conv_kernel, ts=ts, nblk=nblk), grid=(n // ts,), name="conformer_conv",
        in_specs=[pl.BlockSpec((ts, 2 * W_MIX), lambda i: (i, 0)),
                  pl.BlockSpec((HALO, 2 * W_MIX), lambda i: (jnp.maximum(i * hb - 1, 0), 0)),
                  pl.BlockSpec((HALO, 2 * W_MIX), lambda i: (jnp.minimum((i + 1) * hb, last), 0)),
                  pl.BlockSpec((None, CONV_WIDTH, W_MIX), lambda i: (layer, 0, 0)),
                  vec(), vec(), vec(),
                  pl.BlockSpec((None, W_MIX, W_MIX), lambda i: (layer, 0, 0))],
        out_specs=pl.BlockSpec((ts, W_MIX), lambda i: (i, 0)),
        out_shape=jax.ShapeDtypeStruct((n, W_MIX), BF16),
        scratch_shapes=[pltpu.VMEM((ts + 2 * HALO, W_MIX), F32), pltpu.VMEM((7, ts + 2 * HALO, W_MIX), F32)],
        compiler_params=_cparams(("parallel",)),
    )(uc, uc, uc, dw, cb, g, b, wpw_bf)


def _bias_tiles(rel_bias, tq):
    idx = jnp.asarray(_bucket_tiles(tq))
    nbr, _, win = idx.shape
    heads = W_MIX // HEAD_DIM
    return pl.pallas_call(
        _bias_kernel, grid=(nbr,), name="dilated_bias_tiles",
        in_specs=[pl.BlockSpec((None, tq, win), lambda r: (r, 0, 0)),
                  pl.BlockSpec(memory_space=pltpu.SMEM)],
        out_specs=pl.BlockSpec((None, heads, tq, win), lambda r: (r, 0, 0, 0)),
        out_shape=jax.ShapeDtypeStruct((nbr, heads, tq, win), F32),
        compiler_params=_cparams(("parallel",)),
    )(idx, rel_bias)


def _dilated(qd, kd, vd, bias, tq, batch, seq):
    heads = W_MIX // HEAD_DIM
    win = tq + 2 * DIL_N
    tiles, nbr, tm = qd.shape[1], qd.shape[2], qd.shape[3]
    max_dil = max(dil for _, dil in DIL_CONFIGS)
    pad_rows = seq + max_dil * 2 * DIL_N
    assert (seq // tq) % 2 == 0
    assert all(a[1] < b[1] for a, b in zip(DIL_CONFIGS, DIL_CONFIGS[1:]))
    blk = pl.BlockSpec((None, tiles, None, tm, W_MIX), lambda b, r: (b, 0, nbr - 1 - r, 0, 0))
    return pl.pallas_call(
        functools.partial(_dilated_kernel, seq=seq, tm=tm, tq=tq), grid=(batch, nbr), name="dilated_attention",
        in_specs=[blk, blk, blk, pl.BlockSpec((None, heads, tq, win), lambda b, r: (nbr - 1 - r, 0, 0, 0))],
        out_specs=pl.BlockSpec((seq, W_MIX), lambda b, r: (b, 0)),
        out_shape=jax.ShapeDtypeStruct((batch * seq, W_MIX), BF16),
        scratch_shapes=[pltpu.VMEM((seq, W_MIX), BF16), pltpu.VMEM((pad_rows, W_MIX), BF16),
                        pltpu.VMEM((pad_rows, 2 * W_MIX), BF16),
                        pltpu.VMEM((heads, seq, 2 * HEAD_DIM), F32), pltpu.VMEM((heads, seq, 2 * HEAD_DIM), F32),
                        pltpu.VMEM((heads * tq, win), F32), pltpu.VMEM((heads * tq, win), F32)],
        compiler_params=_cparams(("parallel", "arbitrary")),
    )(qd, kd, vd, bias)


def _mix_ffn(x, ya, yb, yc, yd, w_out_bf, g1, b1, w1_bf, w2_bf, g2, b2, layer, seq):
    n = x.shape[0]
    tm = _tile_rows(seq, 1024)
    tf = FFN_CHUNK
    a = seq // FFT_R
    assert FFN_SUB_ROWS % a == 0
    ya_rows = tm // a * _padded_pitch(a)

    def rows(c):
        return pl.BlockSpec((tm, c), lambda i, j: (i, 0))

    def vec():
        return pl.BlockSpec((None, 1, D_MODEL), lambda i, j: (layer, 0, 0))

    return pl.pallas_call(
        functools.partial(_mix_ffn_kernel, nj=D_FF // tf, a=a), grid=(n // tm, D_FF // tf), name="outproj_ffn",
        in_specs=[rows(D_MODEL), pl.BlockSpec((ya_rows, W_MIX), lambda i, j: (i, 0))] + [rows(W_MIX)] * 3
                 + [pl.BlockSpec((None, D_MODEL, D_MODEL), lambda i, j: (layer, 0, 0)), vec(), vec(),
                    pl.BlockSpec((None, D_MODEL, tf), lambda i, j: (layer, 0, j)),
                    pl.BlockSpec((None, tf, D_MODEL), lambda i, j: (layer, j, 0)), vec(), vec()],
        out_specs=rows(D_MODEL), out_shape=jax.ShapeDtypeStruct((n, D_MODEL), F32),
        scratch_shapes=[pltpu.VMEM((tm, D_MODEL), F32), pltpu.VMEM((tm, D_MODEL), F32)],
        compiler_params=_cparams(("parallel", "arbitrary")),
    )(x, ya, yb, yc, yd, w_out_bf, g1, b1, w1_bf, w2_bf, g2, b2)


def kernel(x, emb_ln_g, emb_ln_b, w_in, w_fnet, q_norm_g, k_norm_g, conv_dw, conv_b, conv_ln_g, conv_ln_b,
           w_conv_out, w_out, ln1_g, ln1_b, w_ff1, w_ff2, ln2_g, ln2_b, rel_bias):
    batch, seq, d = x.shape
    depth = w_in.shape[0]
    assert d == D_MODEL and seq % (FFT_R * 16) == 0 and seq % (DIL_CONFIGS[-1][1] * DIL_N) == 0
    n = batch * seq
    a = seq // FFT_R
    heads = W_MIX // HEAD_DIM
    kvw = N_KV_GQA * HEAD_DIM

    cos_np, sin_np = _rope_tables(seq)
    cos, sin = jnp.asarray(cos_np), jnp.asarray(sin_np)
    ones = jnp.asarray(np.kron(np.eye(heads), np.ones((HEAD_DIM, HEAD_DIM))), BF16)
    g1_np, g2_np = _fft_stage_mats(a)
    g1, g2 = jnp.asarray(g1_np, BF16), jnp.asarray(g2_np, BF16)
    twr_np, twi_np = _twiddle(a, seq)
    twr = jnp.broadcast_to(jnp.asarray(twr_np.T)[:, :, None], (FFT_R, a, 128))
    twi = jnp.broadcast_to(jnp.asarray(twi_np.T)[:, :, None], (FFT_R, a, 128))

    w_in_bf = w_in.astype(BF16)
    w_out_bf = w_out.astype(BF16)
    w_ff1_bf = w_ff1.astype(BF16)
    w_ff2_bf = w_ff2.astype(BF16)
    wpw_bf = w_conv_out.astype(BF16)
    gq = jnp.tile(q_norm_g, (1, heads)).reshape(depth, 1, W_MIX)
    gk = jnp.tile(k_norm_g, (1, N_KV_GQA)).reshape(depth, 1, kvw)
    r3 = lambda t: t.reshape(depth, 1, t.shape[-1])

    wfold, wvt_bf = _fold_fourier_weights(w_in, w_fnet, seq)
    tq_d = min(DIL_TQ, seq // MAX_DIL)
    bias = _bias_tiles(rel_bias, tq_d)

    h = _embed_ln(x.reshape(n, d), emb_ln_g, emb_ln_b)
    for l in range(depth):
        zr, zi, qn, kn, v, uc, qd, kd, vd = _inproj(h, wfold, w_in_bf, wvt_bf, cos, sin, gq, gk, ones, l, batch, seq)
        ya = _fourier(zr, zi, g1, g2, twr, twi, batch, seq)
        yb = _gqa(qn, kn, v, batch, seq)
        yd = _dilated(qd, kd, vd, bias, tq_d, batch, seq)
        yc = _conv(uc, conv_dw, r3(conv_b), r3(conv_ln_g), r3(conv_ln_b), wpw_bf, l, seq)
        h = _mix_ffn(h, ya, yb, yc, yd, w_out_bf, r3(ln1_g), r3(ln1_b), w_ff1_bf, w_ff2_bf, r3(ln2_g), r3(ln2_b), l,
                     seq)
    return h.reshape(batch, seq, d)
```

```python
import functools
import math

import numpy as np
import jax
import jax.numpy as jnp
from jax import lax
from jax.experimental import pallas as pl
from jax.experimental.pallas import tpu as pltpu

F32 = jnp.float32
BF16 = jnp.bfloat16

D_MODEL = 1024
DEPTH = 4
HEAD_DIM = 64
W_MIX = 256
N_KV_GQA = 2
CONV_WIDTH = 31
CONV_PAD = CONV_WIDTH // 2
DIL_CONFIGS = ((128, 1), (512, 4), (2048, 16))
DIL_N = 64
MAX_DIL = 16
DIL_TQ = 128
D_FF = 4 * D_MODEL
GRID_W = 64
ROPE_THETA = 10000.0
REL_BUCKETS = 32
REL_MAX_DIST = 1024
ALPHA = (2 * DEPTH) ** 0.25
LN_EPS = 1e-5
RMS_EPS = 1e-6
NEG = -1e30
ATTN_SCALE = HEAD_DIM ** -0.5
LOG2E = math.log2(math.e)

COL_QB, COL_KVB, COL_GLU, COL_QD, COL_KD, COL_VD, D_IN = 256, 512, 768, 1280, 1536, 1792, 2048

LANES = 128
SUBLANES = 8
FFT_R = 64
INPROJ_ROWS = 1024
INPROJ_SUB_ROWS = 1024
GQA_UNIT_LANES = 256
FFN_CHUNK = 2048
FFN_SUB_ROWS = 256
HALO = 16
VMEM_LIMIT = 56 * 1024 * 1024


def _cparams(sem):
    return pltpu.CompilerParams(dimension_semantics=sem, vmem_limit_bytes=VMEM_LIMIT)


def _dft_cos_sin(n):
    j = np.arange(n)
    ang = 2.0 * np.pi * ((j[:, None] * j[None, :]) % n) / n
    return np.cos(ang), np.sin(ang)


def _channel_dft_blockdiag(seq):
    c, s = _dft_cos_sin(HEAD_DIM)
    scale = 1.0 / math.sqrt(seq * HEAD_DIM)
    eye = np.eye(W_MIX // HEAD_DIM)
    return (np.kron(eye, c) * scale).astype(np.float32), (np.kron(eye, -s) * scale).astype(np.float32)


def _fft_stage_mats(a):
    ca, sa = _dft_cos_sin(a)
    g1 = np.block([[ca, sa], [-sa, ca]])
    c2, s2 = _dft_cos_sin(FFT_R)
    g2 = np.concatenate([c2, s2], axis=1)
    return g1.astype(np.float32), g2.astype(np.float32)


def _twiddle(a, seq):
    c = np.arange(a)[:, None]
    b = np.arange(FFT_R)[None, :]
    ang = 2.0 * np.pi * ((b * c) % seq) / seq
    return np.cos(ang).astype(np.float32), (-np.sin(ang)).astype(np.float32)


def _rope_tables(seq):
    rows = seq // GRID_W
    row = np.repeat(np.arange(rows), GRID_W).astype(np.float32)
    col = np.tile(np.arange(GRID_W), rows).astype(np.float32)
    nf = HEAD_DIM // 4
    inv = (ROPE_THETA ** (-np.arange(nf, dtype=np.float32) / nf)).astype(np.float32)
    ar = row[:, None] * inv
    ac = col[:, None] * inv
    cos = np.concatenate([np.cos(ar), np.cos(ar), np.cos(ac), np.cos(ac)], -1)
    sin = np.concatenate([-np.sin(ar), np.sin(ar), -np.sin(ac), np.sin(ac)], -1)
    reps = W_MIX // HEAD_DIM
    return np.tile(cos, (1, reps)).astype(np.float32), np.tile(sin, (1, reps)).astype(np.float32)


def _t5_bucket_np(rel):
    nb = REL_BUCKETS // 2
    max_exact = nb // 2
    ret = np.where(rel > 0, nb, 0)
    n = np.abs(rel)
    nf = np.maximum(n, 1).astype(np.float32)
    large = max_exact + (np.log(nf / np.float32(max_exact)) / np.float32(math.log(REL_MAX_DIST / max_exact))
                         * np.float32(nb - max_exact)).astype(np.int32)
    large = np.minimum(large, nb - 1)
    return ret + np.where(n < max_exact, n, large)


def _bucket_tiles(tq):
    qi = np.arange(tq)[:, None]
    ki = np.arange(tq + 2 * DIL_N)[None, :]
    rel = ki - DIL_N - qi
    tiles = []
    for _, dil in DIL_CONFIGS:
        tiles.append(np.where(np.abs(rel) <= DIL_N, _t5_bucket_np(rel * dil), -1))
    return np.stack(tiles).astype(np.int32)


def _padded_pitch(n):
    return n + SUBLANES


def _layer_norm(r, g, b):
    mu = jnp.mean(r, axis=-1, keepdims=True)
    d = r - mu
    var = jnp.mean(d * d, axis=-1, keepdims=True)
    return d * lax.rsqrt(var + LN_EPS) * g + b


def _dot(a, b):
    return jnp.dot(a, b, preferred_element_type=F32)


def _dot_nt(a, b):
    return lax.dot_general(a, b, (((1,), (1,)), ((), ())), preferred_element_type=F32)


def _norm_rope(x, g, cos, sin, ones):
    x2 = x * x
    hi = x2.astype(BF16)
    lo = (x2 - hi.astype(F32)).astype(BF16)
    ss = _dot(hi, ones) + _dot(lo, ones)
    xn = x * lax.rsqrt(ss * (1.0 / HEAD_DIM) + RMS_EPS) * g
    w = x.shape[1]
    lane = lax.broadcasted_iota(jnp.int32, x.shape, 1)
    quarter = HEAD_DIM // 4
    partner = jnp.where((lane % (2 * quarter)) < quarter,
                        pltpu.roll(xn, w - quarter, 1), pltpu.roll(xn, quarter, 1))
    return xn * cos + partner * sin


def _embed_ln_kernel(x_ref, g_ref, b_ref, o_ref):
    o_ref[...] = _layer_norm(x_ref[...], g_ref[...], b_ref[...])


def _fold_kernel(win_ref, wv_ref, wf_ref, c_ref, s_ref, o_ref, wvt_ref):
    wvt_ref[...] = wv_ref[...].T.astype(BF16)
    hp = lax.Precision.HIGHEST
    wf = wf_ref[...]
    mc = jnp.dot(c_ref[...], wf, precision=hp, preferred_element_type=F32)
    ms = jnp.dot(s_ref[...], wf, precision=hp, preferred_element_type=F32)
    win = win_ref[...]
    o_ref[:, :W_MIX] = jnp.dot(win, mc, precision=hp, preferred_element_type=F32).astype(BF16)
    o_ref[:, W_MIX:] = jnp.dot(win, ms, precision=hp, preferred_element_type=F32).astype(BF16)


def _inproj_kernel(x_ref, wfold_ref, w_ref, wvt_ref, cos_ref, sin_ref, gq_ref, gk_ref, ones_ref,
                   zr_ref, zi_ref, qn_ref, kn_ref, vt_ref, uc_ref, qd_ref, kd_ref, vd_ref, stage_q, stage_k, stage_v,
                   pad_q, pad_k, pad_v):
    kvw = N_KV_GQA * HEAD_DIM
    tm = x_ref.shape[0]
    sub = min(INPROJ_SUB_ROWS, tm)
    nsub = tm // sub
    halves = W_MIX // LANES

    def project(s):
        xb = x_ref[s * sub:(s + 1) * sub, :].astype(BF16)
        return dict(q=_dot(xb, w_ref[:, COL_QB:COL_KVB]), k=_dot(xb, w_ref[:, COL_KVB:COL_KVB + kvw]),
                    z=_dot(xb, wfold_ref[...]), vt=_dot_nt(wvt_ref[...], xb),
                    dil=[_dot(xb, w_ref[:, col:col + W_MIX]) for col in (COL_QD, COL_KD, COL_VD)],
                    uc=_dot(xb, w_ref[:, COL_GLU:COL_QD]))

    def epilogue(s, p):
        rows = slice(s * sub, (s + 1) * sub)
        uc_ref[rows, :] = p["uc"]
        pin = _padded_pitch(FFT_R)
        pad = jnp.zeros((pin - FFT_R, W_MIX), F32)
        for g in range(sub // FFT_R):
            dst = (s * sub // FFT_R + g) * pin
            src = slice(g * FFT_R, (g + 1) * FFT_R)
            for z_ref, lanes in ((zr_ref, slice(0, W_MIX)), (zi_ref, slice(W_MIX, 2 * W_MIX))):
                z_ref[dst:dst + FFT_R, :] = p["z"][src, lanes]
                z_ref[dst + FFT_R:dst + pin, :] = pad
        row = lax.broadcasted_iota(jnp.int32, (HEAD_DIM, sub), 0)
        one_row = jnp.where(row == 0, 1.0, 0.0).astype(BF16)
        vt_ref[:, rows] = jnp.concatenate(
            [piece for g in range(N_KV_GQA)
             for piece in (p["vt"][g * HEAD_DIM:(g + 1) * HEAD_DIM].astype(BF16), one_row)], axis=0)

        for u, is_q, out_ref, stage, stage_pad in zip(p["dil"], (True, False, False), (qd_ref, kd_ref, vd_ref),
                                                      (stage_q, stage_k, stage_v), (pad_q, pad_k, pad_v)):
            if is_q:
                u = u * (ATTN_SCALE * LOG2E)
            for half in range(halves):
                stage[half, rows, :] = u[:, half * LANES:(half + 1) * LANES]
            for br, (_, dil) in enumerate(DIL_CONFIGS):
                if dil == 1:
                    out_ref[br, rows, :] = u.astype(BF16)
                    continue
                c = tm // dil
                cs = sub // dil
                if dil % 8 == 0:
                    pitch = _padded_pitch(dil)
                    for half in range(halves):
                        for j in range(cs):
                            stage_pad[half, j * pitch:j * pitch + dil, :] = u[j * dil:(j + 1) * dil,
                                                                              half * LANES:(half + 1) * LANES]
                    src, start, stride = stage_pad, 0, pitch
                else:
                    src, start, stride = stage, s * sub, dil
                for r in range(dil):
                    out_ref[br, r * c + s * cs:r * c + (s + 1) * cs, :] = jnp.concatenate(
                        [src[half, pl.ds(start + r, cs, stride=stride), :] for half in range(halves)],
                        axis=1).astype(BF16)

        cos = cos_ref[rows, :]
        sin = sin_ref[rows, :]
        ones = ones_ref[...]
        qn_ref[rows, :] = (_norm_rope(p["q"], gq_ref[...], cos, sin, ones) * (ATTN_SCALE * LOG2E)).astype(BF16)
        kn_ref[rows, :] = _norm_rope(p["k"], gk_ref[...], cos[:, :kvw], sin[:, :kvw], ones[:kvw, :kvw]).astype(BF16)

    projected = [project(s) for s in range(nsub)]
    for s in range(nsub):
        epilogue(s, projected[s])


def _fourier_kernel(zr_ref, zi_ref, g1_ref, g2_ref, twr_ref, twi_ref, o_ref, yr_scr, yi_scr, *, a):
    pin = _padded_pitch(FFT_R)
    pmid = _padded_pitch(a)
    g1 = g1_ref[...]
    g2 = g2_ref[...]

    def stage1(b, carry):
        z = jnp.concatenate([zr_ref[pl.ds(b, a, stride=pin), :], zi_ref[pl.ds(b, a, stride=pin), :]], axis=0)
        y = _dot(g1, z.astype(BF16))
        yr, yi = y[:a], y[a:]
        twr = twr_ref[b]
        twi = twi_ref[b]
        row0 = pl.multiple_of(b * pmid, SUBLANES)
        yr_scr[pl.ds(row0, a), :] = yr * twr - yi * twi
        yi_scr[pl.ds(row0, a), :] = yr * twi + yi * twr
        return carry

    lax.fori_loop(0, FFT_R, stage1, 0, unroll=8)

    pad = jnp.zeros((pmid - a, LANES), F32)
    for e in range(FFT_R):
        o_ref[e * pmid + a:(e + 1) * pmid, :] = pad

    def stage2(c, carry):
        y = jnp.concatenate([yr_scr[pl.ds(c, FFT_R, stride=pmid), :], yi_scr[pl.ds(c, FFT_R, stride=pmid), :]], axis=0)
        o_ref[pl.ds(c, FFT_R, stride=pmid), :] = _dot(g2, y.astype(BF16))
        return carry

    lax.fori_loop(0, a, stage2, 0, unroll=8)


def _gqa_kernel(q_ref, k_ref, vt_ref, o_ref, st_a, st_b, *, tq, tk, seq):
    nk = seq // tk
    rep = (W_MIX // HEAD_DIM) // N_KV_GQA
    zeros = jnp.zeros((tq, HEAD_DIM), BF16)
    q2 = []
    for g in range(N_KV_GQA):
        parts = []
        for r in range(rep):
            h = g * rep + r
            qh = q_ref[:, h * HEAD_DIM:(h + 1) * HEAD_DIM]
            parts.append(jnp.concatenate([zeros] * g + [qh] + [zeros] * (N_KV_GQA - 1 - g), axis=1))
        q2.append(jnp.concatenate(parts, axis=0))
    rows = rep * tq
    vw = 2 * HEAD_DIM

    lanes = GQA_UNIT_LANES
    units = [(g, u) for g in range(N_KV_GQA) for u in range(rows // lanes)]
    q_unit = [q2[g][u * lanes:(u + 1) * lanes, :] for g, u in units]

    def scores(c, st_scr, i):
        off = c * tk if isinstance(c, int) else pl.multiple_of(c * tk, tk)
        st_scr[i] = _dot_nt(k_ref[pl.ds(off, tk), :], q_unit[i])

    def softmax_pv(c, st_scr, i, state):
        off = c * tk if isinstance(c, int) else pl.multiple_of(c * tk, tk)
        g = units[i][0]
        m, acc = state
        st = st_scr[i]
        vt = vt_ref[g * vw:(g + 1) * vw, pl.ds(off, tk)]
        m_new = jnp.maximum(m, jnp.max(st, axis=0, keepdims=True))
        alpha = jnp.exp2(m - m_new)
        pt = jnp.exp2(st - m_new).astype(BF16)
        return m_new, alpha * acc + _dot(vt, pt)

    def half(c_cur, c_next, st_cur, st_next, carry):
        out = []
        for i in range(len(units)):
            if c_next is not None:
                scores(c_next, st_next, i)
            out.append(softmax_pv(c_cur, st_cur, i, carry[i]))
        return tuple(out)

    def body(c2, carry):
        c = 2 * c2
        carry = half(c, c + 1, st_a, st_b, carry)
        return half(c + 1, c + 2, st_b, st_a, carry)

    init = tuple((jnp.full((1, lanes), -jnp.inf, F32), jnp.zeros((vw, lanes), F32)) for _ in units)
    for i in range(len(units)):
        scores(0, st_a, i)
    carry = lax.fori_loop(0, nk // 2 - 1, body, init)
    carry = half(nk - 2, nk - 1, st_a, st_b, carry)
    final = half(nk - 1, None, st_b, None, carry)
    for g in range(N_KV_GQA):
        acc = jnp.concatenate([final[i][1] for i, (gi, _) in enumerate(units) if gi == g], axis=1)
        o = (acc[:HEAD_DIM] / acc[HEAD_DIM:HEAD_DIM + 1]).T
        for r in range(rep):
            h = g * rep + r
            o_ref[:, h * HEAD_DIM:(h + 1) * HEAD_DIM] = o[r * tq:(r + 1) * tq].astype(BF16)


def _conv_kernel(cur_ref, prev_ref, next_ref, dw_ref, cb_ref, g_ref, b_ref, wpw_ref, o_ref, hpad, shifted, *, ts, nblk):
    c = pl.program_id(0) % nblk

    def glu(u):
        return u[:, :W_MIX] * jax.nn.sigmoid(u[:, W_MIX:])

    hpad[0:HALO, :] = jnp.where(c > 0, glu(prev_ref[...]), 0.0)
    hpad[HALO:HALO + ts, :] = glu(cur_ref[...])
    hpad[HALO + ts:HALO + ts + HALO, :] = jnp.where(c < nblk - 1, glu(next_ref[...]), 0.0)
    sublanes = SUBLANES
    first = HALO - CONV_PAD
    span = ts + ((first + CONV_WIDTH - 1) // sublanes) * sublanes
    for b in range(1, sublanes):
        shifted[b - 1, 0:span, :] = hpad[b:b + span, :]
    acc = jnp.zeros((ts, W_MIX), F32)
    for j in range(CONV_WIDTH):
        start = first + j
        base = (start // sublanes) * sublanes
        b = start % sublanes
        tap = hpad[base:base + ts, :] if b == 0 else shifted[b - 1, base:base + ts, :]
        acc = acc + dw_ref[j:j + 1, :] * tap
    h = _layer_norm(acc + cb_ref[...], g_ref[...], b_ref[...])
    h = h * jax.nn.sigmoid(h)
    o_ref[...] = _dot(h.astype(BF16), wpw_ref[...]).astype(BF16)


def _bias_kernel(idx_ref, rb_ref, o_ref):
    idx = idx_ref[...]
    for h in range(W_MIX // HEAD_DIM):
        tile = jnp.full(idx.shape, NEG, F32)
        for b in range(REL_BUCKETS):
            tile = jnp.where(idx == b, rb_ref[b, h] * LOG2E, tile)
        o_ref[h] = tile


def _dilated_branch(q_in, k_in, v_in, bias_ref, qbuf, kbuf, vbuf, m_st, acc_st, s_a, s_b,
                    *, dil, first, last, seq, tm, tq):
    n = DIL_N
    heads = W_MIX // HEAD_DIM
    length = seq // dil
    lp = length + 2 * n
    c = tm // dil
    win = tq + 2 * n
    nb = length // tq
    nblocks = dil * nb
    slab = 2 * HEAD_DIM

    lane = lax.broadcasted_iota(jnp.int32, (c, HEAD_DIM), 1)
    one_col = jnp.where(lane == 0, 1.0, 0.0).astype(BF16)
    for r in range(dil):
        base = r * lp
        for buf in (kbuf, vbuf):
            zeros = jnp.zeros((n, buf.shape[1]), BF16)
            buf[base:base + n, :] = zeros
            buf[base + n + length:base + lp, :] = zeros
        for t in range(seq // tm):
            src = slice(r * c, (r + 1) * c)
            qbuf[r * length + t * c:r * length + (t + 1) * c, :] = q_in[t, src, :]
            kbuf[base + n + t * c:base + n + (t + 1) * c, :] = k_in[t, src, :]
            vv = v_in[t, src, :]
            vbuf[base + n + t * c:base + n + (t + 1) * c, :] = jnp.concatenate(
                [piece for h in range(heads) for piece in (vv[:, h * HEAD_DIM:(h + 1) * HEAD_DIM], one_col)], axis=1)

    head_of_lane = lax.broadcasted_iota(jnp.int32, (tq, W_MIX), 1) // HEAD_DIM

    def block_pos(j):
        r = j // nb
        i = j - r * nb
        return r, i, pl.multiple_of(j * tq + r * 2 * n, 2 * n if tq % (2 * n) == 0 else tq)

    def scores(j, s_scr):
        _, _, krow = block_pos(j)
        qb = qbuf[pl.ds(pl.multiple_of(j * tq, tq), tq), :]
        kw = kbuf[pl.ds(krow, win), :]
        qs = jnp.concatenate([jnp.where(head_of_lane == h, qb, jnp.zeros_like(qb)) for h in range(heads)], axis=0)
        s_scr[...] = _dot_nt(qs, kw)

    def update(j, s_scr):
        r, i, krow = block_pos(j)
        vw = vbuf[pl.ds(krow, win), :]
        kpos = i * tq - n + lax.broadcasted_iota(jnp.int32, (tq, win), 1)
        valid = (kpos >= 0) & (kpos < length)
        rows = pl.ds(r + dil * i * tq, tq, stride=dil) if dil > 1 else pl.ds(pl.multiple_of(i * tq, tq), tq)
        for h in range(heads):
            s = jnp.where(valid, s_scr[h * tq:(h + 1) * tq, :] + bias_ref[h], NEG)
            mblk = jnp.max(s, axis=-1, keepdims=True)
            if first:
                m_new = jnp.broadcast_to(mblk, (tq, slab))
            else:
                m_old = m_st[h, rows, :]
                m_new = jnp.maximum(m_old, mblk)
            m_row = jnp.tile(m_new, (1, win // slab)) if win % slab == 0 else m_new[:, :1]
            p = jnp.exp2(s - m_row).astype(BF16)
            pv = _dot(p, vw[:, h * slab:(h + 1) * slab])
            if not first:
                pv = jnp.exp2(m_old - m_new) * acc_st[h, rows, :] + pv
            if not last:
                m_st[h, rows, :] = m_new
            acc_st[h, rows, :] = pv

    def body(j2, carry):
        j = 2 * j2
        scores(j + 1, s_b)
        update(j, s_a)
        scores((j + 2) % nblocks, s_a)
        update(j + 1, s_b)
        return carry

    scores(0, s_a)
    lax.fori_loop(0, nblocks // 2, body, 0)


def _dilated_kernel(q_in, k_in, v_in, bias_ref, o_ref, qbuf, kbuf, vbuf, m_st, acc_st, s_a, s_b, *, seq, tm, tq):
    br = pl.program_id(1)
    nbr = len(DIL_CONFIGS)
    for step in range(nbr):
        dil = DIL_CONFIGS[nbr - 1 - step][1]

        @pl.when(br == step)
        def _(step=step, dil=dil):
            _dilated_branch(q_in, k_in, v_in, bias_ref, qbuf, kbuf, vbuf, m_st, acc_st, s_a, s_b,
                            dil=dil, first=(step == 0), last=(step == nbr - 1), seq=seq, tm=tm, tq=tq)

    @pl.when(br == len(DIL_CONFIGS) - 1)
    def _():
        heads = W_MIX // HEAD_DIM
        rows_per = 256

        def finish(t, carry):
            rows = pl.ds(pl.multiple_of(t * rows_per, rows_per), rows_per)
            outs = []
            for h in range(heads):
                a = acc_st[h, rows, :]
                outs.append(a[:, :HEAD_DIM] / a[:, HEAD_DIM:HEAD_DIM + 1])
            o_ref[rows, :] = jnp.concatenate(outs, axis=1).astype(BF16)
            return carry

        lax.fori_loop(0, seq // rows_per, finish, 0)


def _mix_ffn_kernel(x_ref, ya_ref, yb_ref, yc_ref, yd_ref, wo_ref, g1_ref, b1_ref, w1_ref, w2_ref, g2_ref, b2_ref,
                    o_ref, x1_ref, acc_ref, *, nj, a):
    j = pl.program_id(1)
    sub = FFN_SUB_ROWS
    nsub = x_ref.shape[0] // sub

    def rows(r):
        return slice(r * sub, (r + 1) * sub)

    def mix(r):
        pitch = _padded_pitch(a)
        ya = jnp.concatenate([ya_ref[g * pitch:g * pitch + a, :] for g in range(r * sub // a, (r + 1) * sub // a)],
                             axis=0)
        y_in = jnp.concatenate([ya.astype(BF16), yb_ref[rows(r), :], yc_ref[rows(r), :], yd_ref[rows(r), :]], axis=1)
        x1 = _layer_norm(ALPHA * x_ref[rows(r), :] + _dot(y_in, wo_ref[...]), g1_ref[...], b1_ref[...])
        x1_ref[rows(r), :] = x1
        return x1.astype(BF16)

    def act(h):
        return jnp.square(jnp.maximum(h, 0.0)).astype(BF16)

    def chunk(first, last):
        def lhs(r):
            return mix(r) if first else x1_ref[rows(r), :].astype(BF16)

        h_next = act(_dot(lhs(0), w1_ref[...]))
        for r in range(nsub):
            h_cur = h_next
            if r + 1 < nsub:
                lhs_next = lhs(r + 1)
            y = _dot(h_cur, w2_ref[...])
            if r + 1 < nsub:
                raw = _dot(lhs_next, w1_ref[...])
            if not first:
                y = y + acc_ref[rows(r), :]
            if last:
                o_ref[rows(r), :] = _layer_norm(ALPHA * x1_ref[rows(r), :] + y, g2_ref[...], b2_ref[...])
            else:
                acc_ref[rows(r), :] = y
            if r + 1 < nsub:
                h_next = act(raw)

    if nj == 1:
        chunk(True, True)
    else:
        pl.when(j == 0)(lambda: chunk(True, False))
        if nj > 2:
            pl.when((j > 0) & (j < nj - 1))(lambda: chunk(False, False))
        pl.when(j == nj - 1)(lambda: chunk(False, True))


def _tile_rows(n, want):
    t = min(n, want)
    assert n % t == 0, (n, t)
    return t


def _embed_ln(x, g, b):
    n, d = x.shape
    tm = _tile_rows(n, 1024)
    return pl.pallas_call(
        _embed_ln_kernel, grid=(n // tm,), name="embed_ln",
        in_specs=[pl.BlockSpec((tm, d), lambda i: (i, 0)), pl.BlockSpec((1, d), lambda i: (0, 0)),
                  pl.BlockSpec((1, d), lambda i: (0, 0))],
        out_specs=pl.BlockSpec((tm, d), lambda i: (i, 0)),
        out_shape=jax.ShapeDtypeStruct((n, d), F32), compiler_params=_cparams(("parallel",)),
    )(x, g.reshape(1, d), b.reshape(1, d))


def _fold_fourier_weights(w_in, w_fnet, seq):
    c_bd, s_bd = _channel_dft_blockdiag(seq)
    depth = w_in.shape[0]
    sq = pl.BlockSpec((W_MIX, W_MIX), lambda l: (0, 0))
    kvw = N_KV_GQA * HEAD_DIM
    v_col = COL_KVB + kvw
    assert v_col % kvw == 0
    return pl.pallas_call(
        _fold_kernel, grid=(depth,), name="fold_weights",
        in_specs=[pl.BlockSpec((None, D_MODEL, W_MIX), lambda l: (l, 0, 0)),
                  pl.BlockSpec((None, D_MODEL, kvw), lambda l: (l, 0, v_col // kvw)),
                  pl.BlockSpec((None, W_MIX, W_MIX), lambda l: (l, 0, 0)), sq, sq],
        out_specs=[pl.BlockSpec((None, D_MODEL, 2 * W_MIX), lambda l: (l, 0, 0)),
                   pl.BlockSpec((None, kvw, D_MODEL), lambda l: (l, 0, 0))],
        out_shape=[jax.ShapeDtypeStruct((depth, D_MODEL, 2 * W_MIX), BF16),
                   jax.ShapeDtypeStruct((depth, kvw, D_MODEL), BF16)],
        compiler_params=_cparams(("parallel",)),
    )(w_in, w_in, w_fnet, jnp.asarray(c_bd), jnp.asarray(s_bd))


def _inproj(x, wfold, w_in_bf, wvt_bf, cos, sin, gq, gk, ones, layer, batch, seq):
    n = x.shape[0]
    tm = _tile_rows(seq, INPROJ_ROWS)
    kvw = N_KV_GQA * HEAD_DIM
    pos_blocks = seq // tm
    nbr = len(DIL_CONFIGS)
    assert DIL_CONFIGS[0][1] == 1 and all(tm % (dil * 16) == 0 for _, dil in DIL_CONFIGS)

    def rows(c):
        return pl.BlockSpec((tm, c), lambda i: (i, 0))

    def whole(r, c):
        return pl.BlockSpec((r, c), lambda i: (0, 0))

    def flat(c, dt):
        return rows(c), jax.ShapeDtypeStruct((n, c), dt)

    dil_out = (pl.BlockSpec((None, None, nbr, tm, W_MIX), lambda i: (i // pos_blocks, i % pos_blocks, 0, 0, 0)),
               jax.ShapeDtypeStruct((batch, pos_blocks, nbr, tm, W_MIX), BF16))
    vt_out = (pl.BlockSpec((2 * kvw, tm), lambda i: (0, i)),
              jax.ShapeDtypeStruct((2 * kvw, n), BF16))
    z_rows = tm // FFT_R * _padded_pitch(FFT_R)
    z_out = (pl.BlockSpec((z_rows, W_MIX), lambda i: (i, 0)), jax.ShapeDtypeStruct((n // tm * z_rows, W_MIX), F32))
    outs = [z_out, z_out, flat(W_MIX, BF16), flat(kvw, BF16), vt_out, flat(2 * W_MIX, F32),
            dil_out, dil_out, dil_out]
    return pl.pallas_call(
        _inproj_kernel, grid=(n // tm,), name="inproj",
        in_specs=[rows(D_MODEL),
                  pl.BlockSpec((None, D_MODEL, 2 * W_MIX), lambda i: (layer, 0, 0)),
                  pl.BlockSpec((None, D_MODEL, D_IN), lambda i: (layer, 0, 0)),
                  pl.BlockSpec((None, kvw, D_MODEL), lambda i: (layer, 0, 0)),
                  pl.BlockSpec((tm, W_MIX), lambda i: (i % pos_blocks, 0)),
                  pl.BlockSpec((tm, W_MIX), lambda i: (i % pos_blocks, 0)),
                  pl.BlockSpec((None, 1, W_MIX), lambda i: (layer, 0, 0)),
                  pl.BlockSpec((None, 1, kvw), lambda i: (layer, 0, 0)),
                  whole(W_MIX, W_MIX)],
        out_specs=[o[0] for o in outs], out_shape=[o[1] for o in outs],
        scratch_shapes=[pltpu.VMEM((W_MIX // LANES, tm, LANES), F32)] * 3
                       + [pltpu.VMEM((W_MIX // LANES, tm // MAX_DIL * _padded_pitch(MAX_DIL), LANES), F32)] * 3,
        compiler_params=_cparams(("parallel",)),
    )(x, wfold, w_in_bf, wvt_bf, cos, sin, gq, gk, ones)


def _fourier(zr, zi, g1, g2, twr, twi, batch, seq):
    a = seq // FFT_R
    rows_in = a * _padded_pitch(FFT_R)
    rows_mid = FFT_R * _padded_pitch(a)
    blk_in = pl.BlockSpec((rows_in, LANES), lambda b, j: (b, j))

    def whole(shape):
        return pl.BlockSpec(shape, lambda b, j: (0,) * len(shape))

    return pl.pallas_call(
        functools.partial(_fourier_kernel, a=a), grid=(batch, W_MIX // LANES), name="fourier_mix",
        in_specs=[blk_in, blk_in, whole((2 * a, 2 * a)), whole((FFT_R, 2 * FFT_R)), whole((FFT_R, a, LANES)),
                  whole((FFT_R, a, LANES))],
        out_specs=pl.BlockSpec((rows_mid, LANES), lambda b, j: (b, j)),
        out_shape=jax.ShapeDtypeStruct((batch * rows_mid, W_MIX), F32),
        scratch_shapes=[pltpu.VMEM((rows_mid, LANES), F32)] * 2,
        compiler_params=_cparams(("parallel", "parallel")),
    )(zr, zi, g1, g2, twr, twi)


def _gqa(qn, kn, v, batch, seq):
    tq = _tile_rows(seq, 1024)
    tk = _tile_rows(seq, 256)
    nq = seq // tq
    kvw = N_KV_GQA * HEAD_DIM
    rep = (W_MIX // HEAD_DIM) // N_KV_GQA
    assert (seq // tk) % 2 == 0
    return pl.pallas_call(
        functools.partial(_gqa_kernel, tq=tq, tk=tk, seq=seq), grid=(batch, nq), name="gqa_attention",
        in_specs=[pl.BlockSpec((tq, W_MIX), lambda b, i: (b * nq + i, 0)),
                  pl.BlockSpec((seq, kvw), lambda b, i: (b, 0)),
                  pl.BlockSpec((2 * kvw, seq), lambda b, i: (0, b))],
        out_specs=pl.BlockSpec((tq, W_MIX), lambda b, i: (b * nq + i, 0)),
        out_shape=jax.ShapeDtypeStruct((batch * seq, W_MIX), BF16),
        scratch_shapes=[pltpu.VMEM((N_KV_GQA * rep * tq // GQA_UNIT_LANES, tk, GQA_UNIT_LANES), F32)] * 2,
        compiler_params=_cparams(("parallel", "parallel")),
    )(qn, kn, v)


def _conv(uc, dw, cb, g, b, wpw_bf, layer, seq):
    n = uc.shape[0]
    ts = _tile_rows(seq, 512)
    nblk = seq // ts
    hb = ts // HALO
    last = n // HALO - 1

    def vec():
        return pl.BlockSpec((None, 1, W_MIX), lambda i: (layer, 0, 0))

    return pl.pallas_call(
        functools.partial(_conv_kernel, ts=ts, nblk=nblk), grid=(n // ts,), name="conformer_conv",
        in_specs=[pl.BlockSpec((ts, 2 * W_MIX), lambda i: (i, 0)),
                  pl.BlockSpec((HALO, 2 * W_MIX), lambda i: (jnp.maximum(i * hb - 1, 0), 0)),
                  pl.BlockSpec((HALO, 2 * W_MIX), lambda i: (jnp.minimum((i + 1) * hb, last), 0)),
                  pl.BlockSpec((None, CONV_WIDTH, W_MIX), lambda i: (layer, 0, 0)),
                  vec(), vec(), vec(),
                  pl.BlockSpec((None, W_MIX, W_MIX), lambda i: (layer, 0, 0))],
        out_specs=pl.BlockSpec((ts, W_MIX), lambda i: (i, 0)),
        out_shape=jax.ShapeDtypeStruct((n, W_MIX), BF16),
        scratch_shapes=[pltpu.VMEM((ts + 2 * HALO, W_MIX), F32), pltpu.VMEM((7, ts + 2 * HALO, W_MIX), F32)],
        compiler_params=_cparams(("parallel",)),
    )(uc, uc, uc, dw, cb, g, b, wpw_bf)


def _bias_tiles(rel_bias, tq):
    idx = jnp.asarray(_bucket_tiles(tq))
    nbr, _, win = idx.shape
    heads = W_MIX // HEAD_DIM
    return pl.pallas_call(
        _bias_kernel, grid=(nbr,), name="dilated_bias_tiles",
        in_specs=[pl.BlockSpec((None, tq, win), lambda r: (r, 0, 0)),
                  pl.BlockSpec(memory_space=pltpu.SMEM)],
        out_specs=pl.BlockSpec((None, heads, tq, win), lambda r: (r, 0, 0, 0)),
        out_shape=jax.ShapeDtypeStruct((nbr, heads, tq, win), F32),
        compiler_params=_cparams(("parallel",)),
    )(idx, rel_bias)


def _dilated(qd, kd, vd, bias, tq, batch, seq):
    heads = W_MIX // HEAD_DIM
    win = tq + 2 * DIL_N
    tiles, nbr, tm = qd.shape[1], qd.shape[2], qd.shape[3]
    max_dil = max(dil for _, dil in DIL_CONFIGS)
    pad_rows = seq + max_dil * 2 * DIL_N
    assert (seq // tq) % 2 == 0
    assert all(a[1] < b[1] for a, b in zip(DIL_CONFIGS, DIL_CONFIGS[1:]))
    blk = pl.BlockSpec((None, tiles, None, tm, W_MIX), lambda b, r: (b, 0, nbr - 1 - r, 0, 0))
    return pl.pallas_call(
        functools.partial(_dilated_kernel, seq=seq, tm=tm, tq=tq), grid=(batch, nbr), name="dilated_attention",
        in_specs=[blk, blk, blk, pl.BlockSpec((None, heads, tq, win), lambda b, r: (nbr - 1 - r, 0, 0, 0))],
        out_specs=pl.BlockSpec((seq, W_MIX), lambda b, r: (b, 0)),
        out_shape=jax.ShapeDtypeStruct((batch * seq, W_MIX), BF16),
        scratch_shapes=[pltpu.VMEM((seq, W_MIX), BF16), pltpu.VMEM((pad_rows, W_MIX), BF16),
                        pltpu.VMEM((pad_rows, 2 * W_MIX), BF16),
                        pltpu.VMEM((heads, seq, 2 * HEAD_DIM), F32), pltpu.VMEM((heads, seq, 2 * HEAD_DIM), F32),
                        pltpu.VMEM((heads * tq, win), F32), pltpu.VMEM((heads * tq, win), F32)],
        compiler_params=_cparams(("parallel", "arbitrary")),
    )(qd, kd, vd, bias)


def _mix_ffn(x, ya, yb, yc, yd, w_out_bf, g1, b1, w1_bf, w2_bf, g2, b2, layer, seq):
    n = x.shape[0]
    tm = _tile_rows(seq, 1024)
    tf = FFN_CHUNK
    a = seq // FFT_R
    assert FFN_SUB_ROWS % a == 0
    ya_rows = tm // a * _padded_pitch(a)

    def rows(c):
        return pl.BlockSpec((tm, c), lambda i, j: (i, 0))

    def vec():
        return pl.BlockSpec((None, 1, D_MODEL), lambda i, j: (layer, 0, 0))

    return pl.pallas_call(
        functools.partial(_mix_ffn_kernel, nj=D_FF // tf, a=a), grid=(n // tm, D_FF // tf), name="outproj_ffn",
        in_specs=[rows(D_MODEL), pl.BlockSpec((ya_rows, W_MIX), lambda i, j: (i, 0))] + [rows(W_MIX)] * 3
                 + [pl.BlockSpec((None, D_MODEL, D_MODEL), lambda i, j: (layer, 0, 0)), vec(), vec(),
                    pl.BlockSpec((None, D_MODEL, tf), lambda i, j: (layer, 0, j)),
                    pl.BlockSpec((None, tf, D_MODEL), lambda i, j: (layer, j, 0)), vec(), vec()],
        out_specs=rows(D_MODEL), out_shape=jax.ShapeDtypeStruct((n, D_MODEL), F32),
        scratch_shapes=[pltpu.VMEM((tm, D_MODEL), F32), pltpu.VMEM((tm, D_MODEL), F32)],
        compiler_params=_cparams(("parallel", "arbitrary")),
    )(x, ya, yb, yc, yd, w_out_bf, g1, b1, w1_bf, w2_bf, g2, b2)


def kernel(x, emb_ln_g, emb_ln_b, w_in, w_fnet, q_norm_g, k_norm_g, conv_dw, conv_b, conv_ln_g, conv_ln_b,
           w_conv_out, w_out, ln1_g, ln1_b, w_ff1, w_ff2, ln2_g, ln2_b, rel_bias):
    batch, seq, d = x.shape
    depth = w_in.shape[0]
    assert d == D_MODEL and seq % (FFT_R * 16) == 0 and seq % (DIL_CONFIGS[-1][1] * DIL_N) == 0
    n = batch * seq
    a = seq // FFT_R
    heads = W_MIX // HEAD_DIM
    kvw = N_KV_GQA * HEAD_DIM

    cos_np, sin_np = _rope_tables(seq)
    cos, sin = jnp.asarray(cos_np), jnp.asarray(sin_np)
    ones = jnp.asarray(np.kron(np.eye(heads), np.ones((HEAD_DIM, HEAD_DIM))), BF16)
    g1_np, g2_np = _fft_stage_mats(a)
    g1, g2 = jnp.asarray(g1_np, BF16), jnp.asarray(g2_np, BF16)
    twr_np, twi_np = _twiddle(a, seq)
    twr = jnp.broadcast_to(jnp.asarray(twr_np.T)[:, :, None], (FFT_R, a, LANES))
    twi = jnp.broadcast_to(jnp.asarray(twi_np.T)[:, :, None], (FFT_R, a, LANES))

    w_in_bf = w_in.astype(BF16)
    w_out_bf = w_out.astype(BF16)
    w_ff1_bf = w_ff1.astype(BF16)
    w_ff2_bf = w_ff2.astype(BF16)
    wpw_bf = w_conv_out.astype(BF16)
    gq = jnp.tile(q_norm_g, (1, heads)).reshape(depth, 1, W_MIX)
    gk = jnp.tile(k_norm_g, (1, N_KV_GQA)).reshape(depth, 1, kvw)
    r3 = lambda t: t.reshape(depth, 1, t.shape[-1])

    wfold, wvt_bf = _fold_fourier_weights(w_in, w_fnet, seq)
    tq_d = min(DIL_TQ, seq // MAX_DIL)
    bias = _bias_tiles(rel_bias, tq_d)

    h = _embed_ln(x.reshape(n, d), emb_ln_g, emb_ln_b)
    for l in range(depth):
        zr, zi, qn, kn, v, uc, qd, kd, vd = _inproj(h, wfold, w_in_bf, wvt_bf, cos, sin, gq, gk, ones, l, batch, seq)
        ya = _fourier(zr, zi, g1, g2, twr, twi, batch, seq)
        yb = _gqa(qn, kn, v, batch, seq)
        yd = _dilated(qd, kd, vd, bias, tq_d, batch, seq)
        yc = _conv(uc, conv_dw, r3(conv_b), r3(conv_ln_g), r3(conv_ln_b), wpw_bf, l, seq)
        h = _mix_ffn(h, ya, yb, yc, yd, w_out_bf, r3(ln1_g), r3(ln1_b), w_ff1_bf, w_ff2_bf, r3(ln2_g), r3(ln2_b), l,
                     seq)
    return h.reshape(batch, seq, d)
```

```python
import functools
import math

import numpy as np
import jax
import jax.numpy as jnp
from jax import lax
from jax.experimental import pallas as pl
from jax.experimental.pallas import tpu as pltpu

F32 = jnp.float32
BF16 = jnp.bfloat16

D_MODEL = 1024
DEPTH = 4
HEAD_DIM = 64
W_MIX = 256
N_KV_GQA = 2
CONV_WIDTH = 31
CONV_PAD = CONV_WIDTH // 2
DIL_CONFIGS = ((128, 1), (512, 4), (2048, 16))
DIL_N = 64
MAX_DIL = 16
DIL_TQ = 128
DIL_SCORE_BUFS = 8
D_FF = 4 * D_MODEL
GRID_W = 64
ROPE_THETA = 10000.0
REL_BUCKETS = 32
REL_MAX_DIST = 1024
ALPHA = (2 * DEPTH) ** 0.25
LN_EPS = 1e-5
RMS_EPS = 1e-6
NEG = -1e30
ATTN_SCALE = HEAD_DIM ** -0.5
LOG2E = math.log2(math.e)

COL_QB, COL_KVB, COL_GLU, COL_QD, COL_KD, COL_VD, D_IN = 256, 512, 768, 1280, 1536, 1792, 2048

LANES = 128
SUBLANES = 8
FFT_R = 64
INPROJ_ROWS = 1024
INPROJ_SUB_ROWS = 1024
GQA_UNIT_LANES = 256
FFN_CHUNK = 2048
FFN_SUB_ROWS = 256
HALO = 16
VMEM_LIMIT = 56 * 1024 * 1024


def _cparams(sem):
    return pltpu.CompilerParams(dimension_semantics=sem, vmem_limit_bytes=VMEM_LIMIT)


def _dft_cos_sin(n):
    j = np.arange(n)
    ang = 2.0 * np.pi * ((j[:, None] * j[None, :]) % n) / n
    return np.cos(ang), np.sin(ang)


def _channel_dft_blockdiag(seq):
    c, s = _dft_cos_sin(HEAD_DIM)
    scale = 1.0 / math.sqrt(seq * HEAD_DIM)
    eye = np.eye(W_MIX // HEAD_DIM)
    return (np.kron(eye, c) * scale).astype(np.float32), (np.kron(eye, -s) * scale).astype(np.float32)


def _fft_stage_mats(a):
    ca, sa = _dft_cos_sin(a)
    g1 = np.block([[ca, sa], [-sa, ca]])
    c2, s2 = _dft_cos_sin(FFT_R)
    g2 = np.concatenate([c2, s2], axis=1)
    return g1.astype(np.float32), g2.astype(np.float32)


def _twiddle(a, seq):
    c = np.arange(a)[:, None]
    b = np.arange(FFT_R)[None, :]
    ang = 2.0 * np.pi * ((b * c) % seq) / seq
    return np.cos(ang).astype(np.float32), (-np.sin(ang)).astype(np.float32)


def _rope_tables(seq):
    rows = seq // GRID_W
    row = np.repeat(np.arange(rows), GRID_W).astype(np.float32)
    col = np.tile(np.arange(GRID_W), rows).astype(np.float32)
    nf = HEAD_DIM // 4
    inv = (ROPE_THETA ** (-np.arange(nf, dtype=np.float32) / nf)).astype(np.float32)
    ar = row[:, None] * inv
    ac = col[:, None] * inv
    cos = np.concatenate([np.cos(ar), np.cos(ar), np.cos(ac), np.cos(ac)], -1)
    sin = np.concatenate([-np.sin(ar), np.sin(ar), -np.sin(ac), np.sin(ac)], -1)
    reps = W_MIX // HEAD_DIM
    return np.tile(cos, (1, reps)).astype(np.float32), np.tile(sin, (1, reps)).astype(np.float32)


def _t5_bucket_np(rel):
    nb = REL_BUCKETS // 2
    max_exact = nb // 2
    ret = np.where(rel > 0, nb, 0)
    n = np.abs(rel)
    nf = np.maximum(n, 1).astype(np.float32)
    large = max_exact + (np.log(nf / np.float32(max_exact)) / np.float32(math.log(REL_MAX_DIST / max_exact))
                         * np.float32(nb - max_exact)).astype(np.int32)
    large = np.minimum(large, nb - 1)
    return ret + np.where(n < max_exact, n, large)


def _bucket_tiles(tq):
    qi = np.arange(tq)[:, None]
    ki = np.arange(tq + 2 * DIL_N)[None, :]
    rel = ki - DIL_N - qi
    tiles = []
    for _, dil in DIL_CONFIGS:
        tiles.append(np.where(np.abs(rel) <= DIL_N, _t5_bucket_np(rel * dil), -1))
    return np.stack(tiles).astype(np.int32)


def _padded_pitch(n):
    return n + SUBLANES


def _layer_norm(r, g, b):
    mu = jnp.mean(r, axis=-1, keepdims=True)
    d = r - mu
    var = jnp.mean(d * d, axis=-1, keepdims=True)
    return d * lax.rsqrt(var + LN_EPS) * g + b


def _dot(a, b):
    return jnp.dot(a, b, preferred_element_type=F32)


def _dot_nt(a, b):
    return lax.dot_general(a, b, (((1,), (1,)), ((), ())), preferred_element_type=F32)


def _norm_rope(x, g, cos, sin, ones):
    x2 = x * x
    hi = x2.astype(BF16)
    lo = (x2 - hi.astype(F32)).astype(BF16)
    ss = _dot(hi, ones) + _dot(lo, ones)
    xn = x * lax.rsqrt(ss * (1.0 / HEAD_DIM) + RMS_EPS) * g
    w = x.shape[1]
    lane = lax.broadcasted_iota(jnp.int32, x.shape, 1)
    quarter = HEAD_DIM // 4
    partner = jnp.where((lane % (2 * quarter)) < quarter,
                        pltpu.roll(xn, w - quarter, 1), pltpu.roll(xn, quarter, 1))
    return xn * cos + partner * sin


def _embed_ln_kernel(x_ref, g_ref, b_ref, o_ref):
    o_ref[...] = _layer_norm(x_ref[...], g_ref[...], b_ref[...])


def _fold_kernel(win_ref, wv_ref, wf_ref, c_ref, s_ref, o_ref, wvt_ref):
    wvt_ref[...] = wv_ref[...].T.astype(BF16)
    hp = lax.Precision.HIGHEST
    wf = wf_ref[...]
    mc = jnp.dot(c_ref[...], wf, precision=hp, preferred_element_type=F32)
    ms = jnp.dot(s_ref[...], wf, precision=hp, preferred_element_type=F32)
    win = win_ref[...]
    o_ref[:, :W_MIX] = jnp.dot(win, mc, precision=hp, preferred_element_type=F32).astype(BF16)
    o_ref[:, W_MIX:] = jnp.dot(win, ms, precision=hp, preferred_element_type=F32).astype(BF16)


def _inproj_kernel(x_ref, wfold_ref, w_ref, wvt_ref, cos_ref, sin_ref, gq_ref, gk_ref, ones_ref,
                   zr_ref, zi_ref, qn_ref, kn_ref, vt_ref, uc_ref, qd_ref, kd_ref, vd_ref, stage_q, stage_k, stage_v,
                   pad_q, pad_k, pad_v):
    kvw = N_KV_GQA * HEAD_DIM
    tm = x_ref.shape[0]
    sub = min(INPROJ_SUB_ROWS, tm)
    nsub = tm // sub
    halves = W_MIX // LANES

    def project(s):
        xb = x_ref[s * sub:(s + 1) * sub, :].astype(BF16)
        return dict(q=_dot(xb, w_ref[:, COL_QB:COL_KVB]), k=_dot(xb, w_ref[:, COL_KVB:COL_KVB + kvw]),
                    z=_dot(xb, wfold_ref[...]), vt=_dot_nt(wvt_ref[...], xb),
                    dil=[_dot(xb, w_ref[:, col:col + W_MIX]) for col in (COL_QD, COL_KD, COL_VD)],
                    uc=_dot(xb, w_ref[:, COL_GLU:COL_QD]))

    def epilogue(s, p):
        rows = slice(s * sub, (s + 1) * sub)
        uc_ref[rows, :] = p["uc"]
        pin = _padded_pitch(FFT_R)
        pad = jnp.zeros((pin - FFT_R, W_MIX), F32)
        for g in range(sub // FFT_R):
            dst = (s * sub // FFT_R + g) * pin
            src = slice(g * FFT_R, (g + 1) * FFT_R)
            for z_ref, lanes in ((zr_ref, slice(0, W_MIX)), (zi_ref, slice(W_MIX, 2 * W_MIX))):
                z_ref[dst:dst + FFT_R, :] = p["z"][src, lanes]
                z_ref[dst + FFT_R:dst + pin, :] = pad
        row = lax.broadcasted_iota(jnp.int32, (HEAD_DIM, sub), 0)
        one_row = jnp.where(row == 0, 1.0, 0.0).astype(BF16)
        vt_ref[:, rows] = jnp.concatenate(
            [piece for g in range(N_KV_GQA)
             for piece in (p["vt"][g * HEAD_DIM:(g + 1) * HEAD_DIM].astype(BF16), one_row)], axis=0)

        for u, is_q, out_ref, stage, stage_pad in zip(p["dil"], (True, False, False), (qd_ref, kd_ref, vd_ref),
                                                      (stage_q, stage_k, stage_v), (pad_q, pad_k, pad_v)):
            if is_q:
                u = u * (ATTN_SCALE * LOG2E)
            for half in range(halves):
                stage[half, rows, :] = u[:, half * LANES:(half + 1) * LANES]
            for br, (_, dil) in enumerate(DIL_CONFIGS):
                if dil == 1:
                    out_ref[br, rows, :] = u.astype(BF16)
                    continue
                c = tm // dil
                cs = sub // dil
                if dil % 8 == 0:
                    pitch = _padded_pitch(dil)
                    for half in range(halves):
                        for j in range(cs):
                            stage_pad[half, j * pitch:j * pitch + dil, :] = u[j * dil:(j + 1) * dil,
                                                                              half * LANES:(half + 1) * LANES]
                    src, start, stride = stage_pad, 0, pitch
                else:
                    src, start, stride = stage, s * sub, dil
                for r in range(dil):
                    out_ref[br, r * c + s * cs:r * c + (s + 1) * cs, :] = jnp.concatenate(
                        [src[half, pl.ds(start + r, cs, stride=stride), :] for half in range(halves)],
                        axis=1).astype(BF16)

        cos = cos_ref[rows, :]
        sin = sin_ref[rows, :]
        ones = ones_ref[...]
        qn_ref[rows, :] = (_norm_rope(p["q"], gq_ref[...], cos, sin, ones) * (ATTN_SCALE * LOG2E)).astype(BF16)
        kn_ref[rows, :] = _norm_rope(p["k"], gk_ref[...], cos[:, :kvw], sin[:, :kvw], ones[:kvw, :kvw]).astype(BF16)

    projected = [project(s) for s in range(nsub)]
    for s in range(nsub):
        epilogue(s, projected[s])


def _fourier_kernel(zr_ref, zi_ref, g1_ref, g2_ref, twr_ref, twi_ref, o_ref, yr_scr, yi_scr, *, a):
    pin = _padded_pitch(FFT_R)
    pmid = _padded_pitch(a)
    g1 = g1_ref[...]
    g2 = g2_ref[...]

    def stage1(b, carry):
        z = jnp.concatenate([zr_ref[pl.ds(b, a, stride=pin), :], zi_ref[pl.ds(b, a, stride=pin), :]], axis=0)
        y = _dot(g1, z.astype(BF16))
        yr, yi = y[:a], y[a:]
        twr = twr_ref[b]
        twi = twi_ref[b]
        row0 = pl.multiple_of(b * pmid, SUBLANES)
        yr_scr[pl.ds(row0, a), :] = yr * twr - yi * twi
        yi_scr[pl.ds(row0, a), :] = yr * twi + yi * twr
        return carry

    lax.fori_loop(0, FFT_R, stage1, 0, unroll=8)

    pad = jnp.zeros((pmid - a, LANES), F32)
    for e in range(FFT_R):
        o_ref[e * pmid + a:(e + 1) * pmid, :] = pad

    def stage2(c, carry):
        y = jnp.concatenate([yr_scr[pl.ds(c, FFT_R, stride=pmid), :], yi_scr[pl.ds(c, FFT_R, stride=pmid), :]], axis=0)
        o_ref[pl.ds(c, FFT_R, stride=pmid), :] = _dot(g2, y.astype(BF16))
        return carry

    lax.fori_loop(0, a, stage2, 0, unroll=8)


def _gqa_kernel(q_ref, k_ref, vt_ref, o_ref, st_a, st_b, *, tq, tk, seq):
    nk = seq // tk
    rep = (W_MIX // HEAD_DIM) // N_KV_GQA
    zeros = jnp.zeros((tq, HEAD_DIM), BF16)
    q2 = []
    for g in range(N_KV_GQA):
        parts = []
        for r in range(rep):
            h = g * rep + r
            qh = q_ref[:, h * HEAD_DIM:(h + 1) * HEAD_DIM]
            parts.append(jnp.concatenate([zeros] * g + [qh] + [zeros] * (N_KV_GQA - 1 - g), axis=1))
        q2.append(jnp.concatenate(parts, axis=0))
    rows = rep * tq
    vw = 2 * HEAD_DIM

    lanes = GQA_UNIT_LANES
    units = [(g, u) for g in range(N_KV_GQA) for u in range(rows // lanes)]
    q_unit = [q2[g][u * lanes:(u + 1) * lanes, :] for g, u in units]

    def scores(c, st_scr, i):
        off = c * tk if isinstance(c, int) else pl.multiple_of(c * tk, tk)
        st_scr[i] = _dot_nt(k_ref[pl.ds(off, tk), :], q_unit[i])

    def softmax_pv(c, st_scr, i, state):
        off = c * tk if isinstance(c, int) else pl.multiple_of(c * tk, tk)
        g = units[i][0]
        m, acc = state
        st = st_scr[i]
        vt = vt_ref[g * vw:(g + 1) * vw, pl.ds(off, tk)]
        m_new = jnp.maximum(m, jnp.max(st, axis=0, keepdims=True))
        alpha = jnp.exp2(m - m_new)
        pt = jnp.exp2(st - m_new).astype(BF16)
        return m_new, alpha * acc + _dot(vt, pt)

    def half(c_cur, c_next, st_cur, st_next, carry):
        out = []
        for i in range(len(units)):
            if c_next is not None:
                scores(c_next, st_next, i)
            out.append(softmax_pv(c_cur, st_cur, i, carry[i]))
        return tuple(out)

    def body(c2, carry):
        c = 2 * c2
        carry = half(c, c + 1, st_a, st_b, carry)
        return half(c + 1, c + 2, st_b, st_a, carry)

    init = tuple((jnp.full((1, lanes), -jnp.inf, F32), jnp.zeros((vw, lanes), F32)) for _ in units)
    for i in range(len(units)):
        scores(0, st_a, i)
    carry = lax.fori_loop(0, nk // 2 - 1, body, init)
    carry = half(nk - 2, nk - 1, st_a, st_b, carry)
    final = half(nk - 1, None, st_b, None, carry)
    for g in range(N_KV_GQA):
        acc = jnp.concatenate([final[i][1] for i, (gi, _) in enumerate(units) if gi == g], axis=1)
        o = (acc[:HEAD_DIM] / acc[HEAD_DIM:HEAD_DIM + 1]).T
        for r in range(rep):
            h = g * rep + r
            o_ref[:, h * HEAD_DIM:(h + 1) * HEAD_DIM] = o[r * tq:(r + 1) * tq].astype(BF16)


def _conv_kernel(cur_ref, prev_ref, next_ref, dw_ref, cb_ref, g_ref, b_ref, wpw_ref, o_ref, hpad, shifted, *, ts, nblk):
    c = pl.program_id(0) % nblk

    def glu(u):
        return u[:, :W_MIX] * jax.nn.sigmoid(u[:, W_MIX:])

    hpad[0:HALO, :] = jnp.where(c > 0, glu(prev_ref[...]), 0.0)
    hpad[HALO:HALO + ts, :] = glu(cur_ref[...])
    hpad[HALO + ts:HALO + ts + HALO, :] = jnp.where(c < nblk - 1, glu(next_ref[...]), 0.0)
    sublanes = SUBLANES
    first = HALO - CONV_PAD
    span = ts + ((first + CONV_WIDTH - 1) // sublanes) * sublanes
    for b in range(1, sublanes):
        shifted[b - 1, 0:span, :] = hpad[b:b + span, :]
    acc = jnp.zeros((ts, W_MIX), F32)
    for j in range(CONV_WIDTH):
        start = first + j
        base = (start // sublanes) * sublanes
        b = start % sublanes
        tap = hpad[base:base + ts, :] if b == 0 else shifted[b - 1, base:base + ts, :]
        acc = acc + dw_ref[j:j + 1, :] * tap
    h = _layer_norm(acc + cb_ref[...], g_ref[...], b_ref[...])
    h = h * jax.nn.sigmoid(h)
    o_ref[...] = _dot(h.astype(BF16), wpw_ref[...]).astype(BF16)


def _bias_kernel(idx_ref, rb_ref, o_ref):
    idx = idx_ref[...]
    for h in range(W_MIX // HEAD_DIM):
        tile = jnp.full(idx.shape, NEG, F32)
        for b in range(REL_BUCKETS):
            tile = jnp.where(idx == b, rb_ref[b, h] * LOG2E, tile)
        o_ref[h] = tile


def _dilated_branch(q_in, k_in, v_in, bias_ref, qbuf, kbuf, vbuf, m_st, acc_st, s_bufs,
                    *, dil, first, last, seq, tm, tq):
    n = DIL_N
    heads = W_MIX // HEAD_DIM
    length = seq // dil
    lp = length + 2 * n
    c = tm // dil
    win = tq + 2 * n
    nb = length // tq
    nblocks = dil * nb
    slab = 2 * HEAD_DIM

    lane = lax.broadcasted_iota(jnp.int32, (c, HEAD_DIM), 1)
    one_col = jnp.where(lane == 0, 1.0, 0.0).astype(BF16)
    for r in range(dil):
        base = r * lp
        for buf in (kbuf, vbuf):
            zeros = jnp.zeros((n, buf.shape[1]), BF16)
            buf[base:base + n, :] = zeros
            buf[base + n + length:base + lp, :] = zeros
        for t in range(seq // tm):
            src = slice(r * c, (r + 1) * c)
            qbuf[r * length + t * c:r * length + (t + 1) * c, :] = q_in[t, src, :]
            kbuf[base + n + t * c:base + n + (t + 1) * c, :] = k_in[t, src, :]
            vv = v_in[t, src, :]
            vbuf[base + n + t * c:base + n + (t + 1) * c, :] = jnp.concatenate(
                [piece for h in range(heads) for piece in (vv[:, h * HEAD_DIM:(h + 1) * HEAD_DIM], one_col)], axis=1)

    head_of_lane = lax.broadcasted_iota(jnp.int32, (tq, W_MIX), 1) // HEAD_DIM

    def block_pos(j):
        r = j // nb
        i = j - r * nb
        return r, i, pl.multiple_of(j * tq + r * 2 * n, 2 * n if tq % (2 * n) == 0 else tq)

    def scores(j, s_scr):
        _, _, krow = block_pos(j)
        qb = qbuf[pl.ds(pl.multiple_of(j * tq, tq), tq), :]
        kw = kbuf[pl.ds(krow, win), :]
        qs = jnp.concatenate([jnp.where(head_of_lane == h, qb, jnp.zeros_like(qb)) for h in range(heads)], axis=0)
        s_scr[...] = _dot_nt(qs, kw)

    def update(j, s_scr):
        r, i, krow = block_pos(j)
        vw = vbuf[pl.ds(krow, win), :]
        kpos = i * tq - n + lax.broadcasted_iota(jnp.int32, (tq, win), 1)
        valid = (kpos >= 0) & (kpos < length)
        rows = pl.ds(r + dil * i * tq, tq, stride=dil) if dil > 1 else pl.ds(pl.multiple_of(i * tq, tq), tq)
        for h in range(heads):
            s = jnp.where(valid, s_scr[h * tq:(h + 1) * tq, :] + bias_ref[h], NEG)
            mblk = jnp.max(s, axis=-1, keepdims=True)
            if first:
                m_new = jnp.broadcast_to(mblk, (tq, slab))
            else:
                m_old = m_st[h, rows, :]
                m_new = jnp.maximum(m_old, mblk)
            m_row = jnp.tile(m_new, (1, win // slab)) if win % slab == 0 else m_new[:, :1]
            p = jnp.exp2(s - m_row).astype(BF16)
            pv = _dot(p, vw[:, h * slab:(h + 1) * slab])
            if not first:
                pv = jnp.exp2(m_old - m_new) * acc_st[h, rows, :] + pv
            if not last:
                m_st[h, rows, :] = m_new
            acc_st[h, rows, :] = pv

    nbuf = len(s_bufs)

    def body(jt, carry):
        j = nbuf * jt
        for k in range(nbuf):
            scores((j + k + 1) % nblocks, s_bufs[(k + 1) % nbuf])
            update(j + k, s_bufs[k])
        return carry

    scores(0, s_bufs[0])
    lax.fori_loop(0, nblocks // nbuf, body, 0)


def _dilated_kernel(q_in, k_in, v_in, bias_ref, o_ref, qbuf, kbuf, vbuf, m_st, acc_st, *s_bufs, seq, tm, tq):
    br = pl.program_id(1)
    nbr = len(DIL_CONFIGS)
    for step in range(nbr):
        dil = DIL_CONFIGS[nbr - 1 - step][1]

        @pl.when(br == step)
        def _(step=step, dil=dil):
            _dilated_branch(q_in, k_in, v_in, bias_ref, qbuf, kbuf, vbuf, m_st, acc_st, s_bufs,
                            dil=dil, first=(step == 0), last=(step == nbr - 1), seq=seq, tm=tm, tq=tq)

    @pl.when(br == len(DIL_CONFIGS) - 1)
    def _():
        heads = W_MIX // HEAD_DIM
        rows_per = 256

        def finish(t, carry):
            rows = pl.ds(pl.multiple_of(t * rows_per, rows_per), rows_per)
            outs = []
            for h in range(heads):
                a = acc_st[h, rows, :]
                outs.append(a[:, :HEAD_DIM] / a[:, HEAD_DIM:HEAD_DIM + 1])
            o_ref[rows, :] = jnp.concatenate(outs, axis=1).astype(BF16)
            return carry

        lax.fori_loop(0, seq // rows_per, finish, 0)


def _mix_ffn_kernel(x_ref, ya_ref, yb_ref, yc_ref, yd_ref, wo_ref, g1_ref, b1_ref, w1_ref, w2_ref, g2_ref, b2_ref,
                    o_ref, x1_ref, acc_ref, *, nj, a):
    j = pl.program_id(1)
    sub = FFN_SUB_ROWS
    nsub = x_ref.shape[0] // sub

    def rows(r):
        return slice(r * sub, (r + 1) * sub)

    def mix(r):
        pitch = _padded_pitch(a)
        ya = jnp.concatenate([ya_ref[g * pitch:g * pitch + a, :] for g in range(r * sub // a, (r + 1) * sub // a)],
                             axis=0)
        y_in = jnp.concatenate([ya.astype(BF16), yb_ref[rows(r), :], yc_ref[rows(r), :], yd_ref[rows(r), :]], axis=1)
        x1 = _layer_norm(ALPHA * x_ref[rows(r), :] + _dot(y_in, wo_ref[...]), g1_ref[...], b1_ref[...])
        x1_ref[rows(r), :] = x1
        return x1.astype(BF16)

    def act(h):
        return jnp.square(jnp.maximum(h, 0.0)).astype(BF16)

    def chunk(first, last):
        def lhs(r):
            return mix(r) if first else x1_ref[rows(r), :].astype(BF16)

        h_next = act(_dot(lhs(0), w1_ref[...]))
        for r in range(nsub):
            h_cur = h_next
            if r + 1 < nsub:
                lhs_next = lhs(r + 1)
            y = _dot(h_cur, w2_ref[...])
            if r + 1 < nsub:
                raw = _dot(lhs_next, w1_ref[...])
            if not first:
                y = y + acc_ref[rows(r), :]
            if last:
                o_ref[rows(r), :] = _layer_norm(ALPHA * x1_ref[rows(r), :] + y, g2_ref[...], b2_ref[...])
            else:
                acc_ref[rows(r), :] = y
            if r + 1 < nsub:
                h_next = act(raw)

    if nj == 1:
        chunk(True, True)
    else:
        pl.when(j == 0)(lambda: chunk(True, False))
        if nj > 2:
            pl.when((j > 0) & (j < nj - 1))(lambda: chunk(False, False))
        pl.when(j == nj - 1)(lambda: chunk(False, True))


def _tile_rows(n, want):
    t = min(n, want)
    assert n % t == 0, (n, t)
    return t


def _embed_ln(x, g, b):
    n, d = x.shape
    tm = _tile_rows(n, 1024)
    return pl.pallas_call(
        _embed_ln_kernel, grid=(n // tm,), name="embed_ln",
        in_specs=[pl.BlockSpec((tm, d), lambda i: (i, 0)), pl.BlockSpec((1, d), lambda i: (0, 0)),
                  pl.BlockSpec((1, d), lambda i: (0, 0))],
        out_specs=pl.BlockSpec((tm, d), lambda i: (i, 0)),
        out_shape=jax.ShapeDtypeStruct((n, d), F32), compiler_params=_cparams(("parallel",)),
    )(x, g.reshape(1, d), b.reshape(1, d))


def _fold_fourier_weights(w_in, w_fnet, seq):
    c_bd, s_bd = _channel_dft_blockdiag(seq)
    depth = w_in.shape[0]
    sq = pl.BlockSpec((W_MIX, W_MIX), lambda l: (0, 0))
    kvw = N_KV_GQA * HEAD_DIM
    v_col = COL_KVB + kvw
    assert v_col % kvw == 0
    return pl.pallas_call(
        _fold_kernel, grid=(depth,), name="fold_weights",
        in_specs=[pl.BlockSpec((None, D_MODEL, W_MIX), lambda l: (l, 0, 0)),
                  pl.BlockSpec((None, D_MODEL, kvw), lambda l: (l, 0, v_col // kvw)),
                  pl.BlockSpec((None, W_MIX, W_MIX), lambda l: (l, 0, 0)), sq, sq],
        out_specs=[pl.BlockSpec((None, D_MODEL, 2 * W_MIX), lambda l: (l, 0, 0)),
                   pl.BlockSpec((None, kvw, D_MODEL), lambda l: (l, 0, 0))],
        out_shape=[jax.ShapeDtypeStruct((depth, D_MODEL, 2 * W_MIX), BF16),
                   jax.ShapeDtypeStruct((depth, kvw, D_MODEL), BF16)],
        compiler_params=_cparams(("parallel",)),
    )(w_in, w_in, w_fnet, jnp.asarray(c_bd), jnp.asarray(s_bd))


def _inproj(x, wfold, w_in_bf, wvt_bf, cos, sin, gq, gk, ones, layer, batch, seq):
    n = x.shape[0]
    tm = _tile_rows(seq, INPROJ_ROWS)
    kvw = N_KV_GQA * HEAD_DIM
    pos_blocks = seq // tm
    nbr = len(DIL_CONFIGS)
    assert DIL_CONFIGS[0][1] == 1 and all(tm % (dil * 16) == 0 for _, dil in DIL_CONFIGS)

    def rows(c):
        return pl.BlockSpec((tm, c), lambda i: (i, 0))

    def whole(r, c):
        return pl.BlockSpec((r, c), lambda i: (0, 0))

    def flat(c, dt):
        return rows(c), jax.ShapeDtypeStruct((n, c), dt)

    dil_out = (pl.BlockSpec((None, None, nbr, tm, W_MIX), lambda i: (i // pos_blocks, i % pos_blocks, 0, 0, 0)),
               jax.ShapeDtypeStruct((batch, pos_blocks, nbr, tm, W_MIX), BF16))
    vt_out = (pl.BlockSpec((2 * kvw, tm), lambda i: (0, i)),
              jax.ShapeDtypeStruct((2 * kvw, n), BF16))
    z_rows = tm // FFT_R * _padded_pitch(FFT_R)
    z_out = (pl.BlockSpec((z_rows, W_MIX), lambda i: (i, 0)), jax.ShapeDtypeStruct((n // tm * z_rows, W_MIX), F32))
    outs = [z_out, z_out, flat(W_MIX, BF16), flat(kvw, BF16), vt_out, flat(2 * W_MIX, F32),
            dil_out, dil_out, dil_out]
    return pl.pallas_call(
        _inproj_kernel, grid=(n // tm,), name="inproj",
        in_specs=[rows(D_MODEL),
                  pl.BlockSpec((None, D_MODEL, 2 * W_MIX), lambda i: (layer, 0, 0)),
                  pl.BlockSpec((None, D_MODEL, D_IN), lambda i: (layer, 0, 0)),
                  pl.BlockSpec((None, kvw, D_MODEL), lambda i: (layer, 0, 0)),
                  pl.BlockSpec((tm, W_MIX), lambda i: (i % pos_blocks, 0)),
                  pl.BlockSpec((tm, W_MIX), lambda i: (i % pos_blocks, 0)),
                  pl.BlockSpec((None, 1, W_MIX), lambda i: (layer, 0, 0)),
                  pl.BlockSpec((None, 1, kvw), lambda i: (layer, 0, 0)),
                  whole(W_MIX, W_MIX)],
        out_specs=[o[0] for o in outs], out_shape=[o[1] for o in outs],
        scratch_shapes=[pltpu.VMEM((W_MIX // LANES, tm, LANES), F32)] * 3
                       + [pltpu.VMEM((W_MIX // LANES, tm // MAX_DIL * _padded_pitch(MAX_DIL), LANES), F32)] * 3,
        compiler_params=_cparams(("parallel",)),
    )(x, wfold, w_in_bf, wvt_bf, cos, sin, gq, gk, ones)


def _fourier(zr, zi, g1, g2, twr, twi, batch, seq):
    a = seq // FFT_R
    rows_in = a * _padded_pitch(FFT_R)
    rows_mid = FFT_R * _padded_pitch(a)
    blk_in = pl.BlockSpec((rows_in, LANES), lambda b, j: (b, j))

    def whole(shape):
        return pl.BlockSpec(shape, lambda b, j: (0,) * len(shape))

    return pl.pallas_call(
        functools.partial(_fourier_kernel, a=a), grid=(batch, W_MIX // LANES), name="fourier_mix",
        in_specs=[blk_in, blk_in, whole((2 * a, 2 * a)), whole((FFT_R, 2 * FFT_R)), whole((FFT_R, a, LANES)),
                  whole((FFT_R, a, LANES))],
        out_specs=pl.BlockSpec((rows_mid, LANES), lambda b, j: (b, j)),
        out_shape=jax.ShapeDtypeStruct((batch * rows_mid, W_MIX), F32),
        scratch_shapes=[pltpu.VMEM((rows_mid, LANES), F32)] * 2,
        compiler_params=_cparams(("parallel", "parallel")),
    )(zr, zi, g1, g2, twr, twi)


def _gqa(qn, kn, v, batch, seq):
    tq = _tile_rows(seq, 1024)
    tk = _tile_rows(seq, 256)
    nq = seq // tq
    kvw = N_KV_GQA * HEAD_DIM
    rep = (W_MIX // HEAD_DIM) // N_KV_GQA
    assert (seq // tk) % 2 == 0
    return pl.pallas_call(
        functools.partial(_gqa_kernel, tq=tq, tk=tk, seq=seq), grid=(batch, nq), name="gqa_attention",
        in_specs=[pl.BlockSpec((tq, W_MIX), lambda b, i: (b * nq + i, 0)),
                  pl.BlockSpec((seq, kvw), lambda b, i: (b, 0)),
                  pl.BlockSpec((2 * kvw, seq), lambda b, i: (0, b))],
        out_specs=pl.BlockSpec((tq, W_MIX), lambda b, i: (b * nq + i, 0)),
        out_shape=jax.ShapeDtypeStruct((batch * seq, W_MIX), BF16),
        scratch_shapes=[pltpu.VMEM((N_KV_GQA * rep * tq // GQA_UNIT_LANES, tk, GQA_UNIT_LANES), F32)] * 2,
        compiler_params=_cparams(("parallel", "parallel")),
    )(qn, kn, v)


def _conv(uc, dw, cb, g, b, wpw_bf, layer, seq):
    n = uc.shape[0]
    ts = _tile_rows(seq, 512)
    nblk = seq // ts
    hb = ts // HALO
    last = n // HALO - 1

    def vec():
        return pl.BlockSpec((None, 1, W_MIX), lambda i: (layer, 0, 0))

    return pl.pallas_call(
        functools.partial(_conv_kernel, ts=ts, nblk=nblk), grid=(n // ts,), name="conformer_conv",
        in_specs=[pl.BlockSpec((ts, 2 * W_MIX), lambda i: (i, 0)),
                  pl.BlockSpec((HALO, 2 * W_MIX), lambda i: (jnp.maximum(i * hb - 1, 0), 0)),
                  pl.BlockSpec((HALO, 2 * W_MIX), lambda i: (jnp.minimum((i + 1) * hb, last), 0)),
                  pl.BlockSpec((None, CONV_WIDTH, W_MIX), lambda i: (layer, 0, 0)),
                  vec(), vec(), vec(),
                  pl.BlockSpec((None, W_MIX, W_MIX), lambda i: (layer, 0, 0))],
        out_specs=pl.BlockSpec((ts, W_MIX), lambda i: (i, 0)),
        out_shape=jax.ShapeDtypeStruct((n, W_MIX), BF16),
        scratch_shapes=[pltpu.VMEM((ts + 2 * HALO, W_MIX), F32), pltpu.VMEM((7, ts + 2 * HALO, W_MIX), F32)],
        compiler_params=_cparams(("parallel",)),
    )(uc, uc, uc, dw, cb, g, b, wpw_bf)


def _bias_tiles(rel_bias, tq):
    idx = jnp.asarray(_bucket_tiles(tq))
    nbr, _, win = idx.shape
    heads = W_MIX // HEAD_DIM
    return pl.pallas_call(
        _bias_kernel, grid=(nbr,), name="dilated_bias_tiles",
        in_specs=[pl.BlockSpec((None, tq, win), lambda r: (r, 0, 0)),
                  pl.BlockSpec(memory_space=pltpu.SMEM)],
        out_specs=pl.BlockSpec((None, heads, tq, win), lambda r: (r, 0, 0, 0)),
        out_shape=jax.ShapeDtypeStruct((nbr, heads, tq, win), F32),
        compiler_params=_cparams(("parallel",)),
    )(idx, rel_bias)


def _dilated(qd, kd, vd, bias, tq, batch, seq):
    heads = W_MIX // HEAD_DIM
    win = tq + 2 * DIL_N
    tiles, nbr, tm = qd.shape[1], qd.shape[2], qd.shape[3]
    max_dil = max(dil for _, dil in DIL_CONFIGS)
    pad_rows = seq + max_dil * 2 * DIL_N
    assert (seq // tq) % DIL_SCORE_BUFS == 0
    assert all(a[1] < b[1] for a, b in zip(DIL_CONFIGS, DIL_CONFIGS[1:]))
    blk = pl.BlockSpec((None, tiles, None, tm, W_MIX), lambda b, r: (b, 0, nbr - 1 - r, 0, 0))
    return pl.pallas_call(
        functools.partial(_dilated_kernel, seq=seq, tm=tm, tq=tq), grid=(batch, nbr), name="dilated_attention",
        in_specs=[blk, blk, blk, pl.BlockSpec((None, heads, tq, win), lambda b, r: (nbr - 1 - r, 0, 0, 0))],
        out_specs=pl.BlockSpec((seq, W_MIX), lambda b, r: (b, 0)),
        out_shape=jax.ShapeDtypeStruct((batch * seq, W_MIX), BF16),
        scratch_shapes=[pltpu.VMEM((seq, W_MIX), BF16), pltpu.VMEM((pad_rows, W_MIX), BF16),
                        pltpu.VMEM((pad_rows, 2 * W_MIX), BF16),
                        pltpu.VMEM((heads, seq, 2 * HEAD_DIM), F32), pltpu.VMEM((heads, seq, 2 * HEAD_DIM), F32),
                        ] + [pltpu.VMEM((heads * tq, win), F32)] * DIL_SCORE_BUFS,
        compiler_params=_cparams(("parallel", "arbitrary")),
    )(qd, kd, vd, bias)


def _mix_ffn(x, ya, yb, yc, yd, w_out_bf, g1, b1, w1_bf, w2_bf, g2, b2, layer, seq):
    n = x.shape[0]
    tm = _tile_rows(seq, 1024)
    tf = FFN_CHUNK
    a = seq // FFT_R
    assert FFN_SUB_ROWS % a == 0
    ya_rows = tm // a * _padded_pitch(a)

    def rows(c):
        return pl.BlockSpec((tm, c), lambda i, j: (i, 0))

    def vec():
        return pl.BlockSpec((None, 1, D_MODEL), lambda i, j: (layer, 0, 0))

    return pl.pallas_call(
        functools.partial(_mix_ffn_kernel, nj=D_FF // tf, a=a), grid=(n // tm, D_FF // tf), name="outproj_ffn",
        in_specs=[rows(D_MODEL), pl.BlockSpec((ya_rows, W_MIX), lambda i, j: (i, 0))] + [rows(W_MIX)] * 3
                 + [pl.BlockSpec((None, D_MODEL, D_MODEL), lambda i, j: (layer, 0, 0)), vec(), vec(),
                    pl.BlockSpec((None, D_MODEL, tf), lambda i, j: (layer, 0, j)),
                    pl.BlockSpec((None, tf, D_MODEL), lambda i, j: (layer, j, 0)), vec(), vec()],
        out_specs=rows(D_MODEL), out_shape=jax.ShapeDtypeStruct((n, D_MODEL), F32),
        scratch_shapes=[pltpu.VMEM((tm, D_MODEL), F32), pltpu.VMEM((tm, D_MODEL), F32)],
        compiler_params=_cparams(("parallel", "arbitrary")),
    )(x, ya, yb, yc, yd, w_out_bf, g1, b1, w1_bf, w2_bf, g2, b2)


def kernel(x, emb_ln_g, emb_ln_b, w_in, w_fnet, q_norm_g, k_norm_g, conv_dw, conv_b, conv_ln_g, conv_ln_b,
           w_conv_out, w_out, ln1_g, ln1_b, w_ff1, w_ff2, ln2_g, ln2_b, rel_bias):
    batch, seq, d = x.shape
    depth = w_in.shape[0]
    assert d == D_MODEL and seq % (FFT_R * 16) == 0 and seq % (DIL_CONFIGS[-1][1] * DIL_N) == 0
    n = batch * seq
    a = seq // FFT_R
    heads = W_MIX // HEAD_DIM
    kvw = N_KV_GQA * HEAD_DIM

    cos_np, sin_np = _rope_tables(seq)
    cos, sin = jnp.asarray(cos_np), jnp.asarray(sin_np)
    ones = jnp.asarray(np.kron(np.eye(heads), np.ones((HEAD_DIM, HEAD_DIM))), BF16)
    g1_np, g2_np = _fft_stage_mats(a)
    g1, g2 = jnp.asarray(g1_np, BF16), jnp.asarray(g2_np, BF16)
    twr_np, twi_np = _twiddle(a, seq)
    twr = jnp.broadcast_to(jnp.asarray(twr_np.T)[:, :, None], (FFT_R, a, LANES))
    twi = jnp.broadcast_to(jnp.asarray(twi_np.T)[:, :, None], (FFT_R, a, LANES))

    w_in_bf = w_in.astype(BF16)
    w_out_bf = w_out.astype(BF16)
    w_ff1_bf = w_ff1.astype(BF16)
    w_ff2_bf = w_ff2.astype(BF16)
    wpw_bf = w_conv_out.astype(BF16)
    gq = jnp.tile(q_norm_g, (1, heads)).reshape(depth, 1, W_MIX)
    gk = jnp.tile(k_norm_g, (1, N_KV_GQA)).reshape(depth, 1, kvw)
    r3 = lambda t: t.reshape(depth, 1, t.shape[-1])

    wfold, wvt_bf = _fold_fourier_weights(w_in, w_fnet, seq)
    tq_d = min(DIL_TQ, seq // MAX_DIL)
    bias = _bias_tiles(rel_bias, tq_d)

    h = _embed_ln(x.reshape(n, d), emb_ln_g, emb_ln_b)
    for l in range(depth):
        zr, zi, qn, kn, v, uc, qd, kd, vd = _inproj(h, wfold, w_in_bf, wvt_bf, cos, sin, gq, gk, ones, l, batch, seq)
        ya = _fourier(zr, zi, g1, g2, twr, twi, batch, seq)
        yb = _gqa(qn, kn, v, batch, seq)
        yd = _dilated(qd, kd, vd, bias, tq_d, batch, seq)
        yc = _conv(uc, conv_dw, r3(conv_b), r3(conv_ln_g), r3(conv_ln_b), wpw_bf, l, seq)
        h = _mix_ffn(h, ya, yb, yc, yd, w_out_bf, r3(ln1_g), r3(ln1_b), w_ff1_bf, w_ff2_bf, r3(ln2_g), r3(ln2_b), l,
                     seq)
    return h.reshape(batch, seq, d)
```
